```python
import jax
import jax.numpy as jnp
from jax import lax
import numpy as np

D_MODEL = 1024
BATCH = 4
SEQ = 8192
DEPTH = 1
DEC_BATCH = 32
DEC_SEQ = 8
PAST_LEN = 16384
PAGE_SIZE = 128

D_CONV = D_MODEL // 2
CONV_K = 31
N_HEADS = 8
N_KV_HEADS = 2
HEAD_DIM = 64
GROUP = N_HEADS // N_KV_HEADS
Q_W = N_HEADS * HEAD_DIM
KV_W = N_KV_HEADS * HEAD_DIM
BLOCK = 64
N_SEL = 16
WINDOW = 512
ROPE_THETA = 10000.0
D_FF = ((8 * D_MODEL + 3 * 256 - 1) // (3 * 256)) * 256
Q_BLOCK = 128
EPS = 1e-6
FORCED_BONUS = 2.0 * GROUP
OFF_Q = 2 * D_CONV
OFF_KV = OFF_Q + Q_W
OFF_NG = OFF_KV + 6 * KV_W
OFF_MG = OFF_NG + 3 * N_HEADS
D_IN = OFF_MG + 2 * D_MODEL

kernel_name = 'hybrid_conformer_nsa_adaln_decoder_step'


def rms_norm(x, g):
    xf = x.astype(jnp.float32)
    y = xf * lax.rsqrt(jnp.mean(xf * xf, axis=-1, keepdims=True) + EPS)
    return (y * g.astype(jnp.float32)).astype(x.dtype)


def layer_norm(x, g, b):
    xf = x.astype(jnp.float32)
    mu = jnp.mean(xf, axis=-1, keepdims=True)
    var = jnp.mean(jnp.square(xf - mu), axis=-1, keepdims=True)
    y = (xf - mu) * lax.rsqrt(var + EPS)
    return (y * g.astype(jnp.float32) + b.astype(jnp.float32)).astype(x.dtype)


def rope(x, pos):
    half = HEAD_DIM // 2
    inv = jnp.power(ROPE_THETA, -jnp.arange(half, dtype=jnp.float32) / half)
    ang = pos.astype(jnp.float32)[:, None] * inv[None, :]
    cos = jnp.cos(ang)[None, :, None, :]
    sin = jnp.sin(ang)[None, :, None, :]
    xf = x.astype(jnp.float32)
    x1, x2 = xf[..., :half], xf[..., half:]
    return jnp.concatenate([x1 * cos - x2 * sin, x2 * cos + x1 * sin], axis=-1).astype(x.dtype)


def masked_softmax(s, mask):
    s = jnp.where(mask, s, -jnp.inf)
    m = jnp.max(s, axis=-1, keepdims=True)
    m = jnp.where(jnp.isfinite(m), m, 0.0)
    e = jnp.where(mask, jnp.exp(s - m), 0.0)
    return e / jnp.maximum(jnp.sum(e, axis=-1, keepdims=True), 1e-30)


def compress(rows, pos_mod, w):
    B, L, H, D = rows.shape
    blocks = rows.reshape(B, L // BLOCK, BLOCK, H, D)
    summ = jnp.mean((blocks * (1.0 + pos_mod)[None, None, :, None, :]).astype(jnp.float32), axis=2)
    return jnp.einsum('bnhd,de->bnhe', summ.astype(rows.dtype), w)


def front(x, c, pos, p):
    B, T, _ = x.shape
    mod = jax.nn.silu(c) @ p['w_ada'] + p['b_ada']
    sh1, sc1, g1, sh2, sc2, g2 = jnp.split(mod[:, None, :], 6, axis=-1)
    h = rms_norm(x, p['mix_norm_g']) * (1.0 + sc1) + sh1
    z = h @ p['w_in']
    glu, q, kv, ng, mg = jnp.split(z, [OFF_Q, OFF_KV, OFF_NG, OFF_MG], axis=-1)
    glu_a, glu_b = jnp.split(glu, 2, axis=-1)
    u = glu_a * jax.nn.sigmoid(glu_b)
    q = rms_norm(q.reshape(B, T, N_HEADS, HEAD_DIM), p['q_norm_g'])
    q_plain = q.reshape(B, T, N_KV_HEADS, GROUP, HEAD_DIM)
    q_rope = rope(q, pos).reshape(B, T, N_KV_HEADS, GROUP, HEAD_DIM)
    kv = kv.reshape(B, T, 6, N_KV_HEADS, HEAD_DIM)
    kng = p['k_norm_g']
    g_conv, g_nsa = jnp.split(jax.nn.sigmoid(mg), 2, axis=-1)
    return dict(
        g1=g1, sh2=sh2, sc2=sc2, g2=g2, u=u, q_plain=q_plain, q_rope=q_rope,
        k_cmp=rms_norm(kv[:, :, 0], kng[0]), v_cmp=kv[:, :, 1],
        k_sel=rope(rms_norm(kv[:, :, 2], kng[1]), pos), v_sel=kv[:, :, 3],
        k_win=rope(rms_norm(kv[:, :, 4], kng[2]), pos), v_win=kv[:, :, 5],
        branch_g=jax.nn.sigmoid(ng).reshape(B, T, N_KV_HEADS, GROUP, 3),
        g_conv=g_conv, g_nsa=g_nsa)


def conv_branch(u, u_prev, p):
    ext = jnp.concatenate([u_prev.astype(u.dtype), u], axis=1)
    y = lax.conv_general_dilated(ext, p['w_dw'][:, None, :].astype(u.dtype), window_strides=(1,),
                                 padding='VALID', dimension_numbers=('NWC', 'WIO', 'NWC'),
                                 feature_group_count=D_CONV) + p['b_dw']
    y = jax.nn.silu(layer_norm(y, p['conv_ln_g'], p['conv_ln_b']))
    return y @ p['w_pw2'], ext[:, -(CONV_K - 1):]


def nsa_attend(q_plain, q_rope, qpos, kc, vc, k_pool, v_pool, block_table, kw, vw, kwpos, branch_g):
    B, Q = q_plain.shape[:2]
    NB = kc.shape[1]
    scale = HEAD_DIM ** -0.5
    blk = jnp.arange(NB, dtype=jnp.int32)
    cur = qpos // BLOCK
    s_cmp = jnp.einsum('bqhgd,bnhd->bqhgn', q_plain, kc).astype(jnp.float32) * scale
    cmp_ok = (blk[None, :] + 1) * BLOCK <= qpos[:, None] + 1
    p_cmp = masked_softmax(s_cmp, cmp_ok[None, :, None, None, :])
    o_cmp = jnp.einsum('bqhgn,bnhd->bqhgd', p_cmp.astype(vc.dtype), vc)
    imp = jnp.sum(p_cmp, axis=3)
    visible = blk[None, :] <= cur[:, None]
    forced = (blk[None, :] == 0) | (blk[None, :] == cur[:, None]) | (blk[None, :] == cur[:, None] - 1)
    bonus = jnp.where(forced, FORCED_BONUS, 0.0)
    score = jnp.where(visible[None, :, None, :], imp + bonus[None, :, None, :], -1.0)
    _, idx = lax.top_k(score, min(N_SEL, NB))
    n_top = idx.shape[-1]
    phys = jax.vmap(lambda bt, i: bt[i])(block_table, idx)
    hidx = jnp.arange(N_KV_HEADS)[None, None, :, None]
    kg = k_pool[phys, :, hidx]
    vg = v_pool[phys, :, hidx]
    kpos = idx[..., None] * BLOCK + jnp.arange(BLOCK, dtype=jnp.int32)
    sel_ok = (kpos <= qpos[None, :, None, None, None]).reshape(B, Q, N_KV_HEADS, 1, n_top * BLOCK)
    s_sel = jnp.einsum('bqhgd,bqhnkd->bqhgnk', q_rope, kg).astype(jnp.float32)
    s_sel = s_sel.reshape(B, Q, N_KV_HEADS, GROUP, n_top * BLOCK) * scale
    p_sel = masked_softmax(s_sel, sel_ok)
    o_sel = jnp.einsum('bqhgm,bqhmd->bqhgd', p_sel.astype(vg.dtype),
                       vg.reshape(B, Q, N_KV_HEADS, n_top * BLOCK, HEAD_DIM))
    s_win = jnp.einsum('bqhgd,blhd->bqhgl', q_rope, kw).astype(jnp.float32) * scale
    win_ok = ((kwpos[None, :] <= qpos[:, None]) & (kwpos[None, :] > qpos[:, None] - WINDOW)
              & (kwpos[None, :] >= 0))
    p_win = masked_softmax(s_win, win_ok[None, :, None, None, :])
    o_win = jnp.einsum('bqhgl,blhd->bqhgd', p_win.astype(vw.dtype), vw)
    g = branch_g
    return g[..., 0:1] * o_cmp + g[..., 1:2] * o_sel + g[..., 2:3] * o_win


def back(x, f, u_out, o, p):
    B, T = x.shape[:2]
    nsa = o.reshape(B, T, Q_W) @ p['w_nsa_o']
    mix = (f['g_conv'] * u_out + f['g_nsa'] * nsa) @ p['w_out']
    x = x + f['g1'] * mix
    h = rms_norm(x, p['ffn_norm_g']) * (1.0 + f['sc2']) + f['sh2']
    ffn = (jax.nn.silu(h @ p['w_gate']) * (h @ p['w_up'])) @ p['w_down']
    return x + f['g2'] * ffn


def prompt_layer(x, c, p):
    B, T, _ = x.shape
    pos = jnp.arange(T, dtype=jnp.int32)
    f = front(x, c, pos, p)
    u_out, conv_state = conv_branch(f['u'], jnp.zeros((B, CONV_K - 1, D_CONV), x.dtype), p)
    kc = compress(f['k_cmp'], p['cmp_mod_k'], p['cmp_w_k'])
    vc = compress(f['v_cmp'], p['cmp_mod_v'], p['cmp_w_v'])
    nb = T // BLOCK
    k_pool = f['k_sel'].reshape(B * nb, BLOCK, N_KV_HEADS, HEAD_DIM)
    v_pool = f['v_sel'].reshape(B * nb, BLOCK, N_KV_HEADS, HEAD_DIM)
    block_table = jnp.arange(B * nb, dtype=jnp.int32).reshape(B, nb)
    pad = ((0, 0), (WINDOW, 0), (0, 0), (0, 0))
    kw_pad = jnp.pad(f['k_win'], pad)
    vw_pad = jnp.pad(f['v_win'], pad)

    def one_block(i):
        s0 = i * Q_BLOCK
        qpos = s0 + jnp.arange(Q_BLOCK, dtype=jnp.int32)
        sl = lambda a: lax.dynamic_slice_in_dim(a, s0, Q_BLOCK, axis=1)
        kw = lax.dynamic_slice_in_dim(kw_pad, s0, WINDOW + Q_BLOCK, axis=1)
        vw = lax.dynamic_slice_in_dim(vw_pad, s0, WINDOW + Q_BLOCK, axis=1)
        kwpos = s0 - WINDOW + jnp.arange(WINDOW + Q_BLOCK, dtype=jnp.int32)
        return nsa_attend(sl(f['q_plain']), sl(f['q_rope']), qpos, kc, vc, k_pool, v_pool,
                          block_table, kw, vw, kwpos, sl(f['branch_g']))

    o = lax.map(one_block, jnp.arange(T // Q_BLOCK, dtype=jnp.int32))
    o = jnp.moveaxis(o, 0, 1).reshape(B, T, N_KV_HEADS, GROUP, HEAD_DIM)
    y = back(x, f, u_out, o, p)
    keep = min(WINDOW, T)
    return y, (f['k_cmp'], f['v_cmp'], f['k_sel'], f['v_sel'],
               f['k_win'][:, -keep:], f['v_win'][:, -keep:], conv_state)


def sample_layer(x, c, cache_cmp_k, cache_cmp_v, cache_sel_k, cache_sel_v,
                 cache_win_k, cache_win_v, state_conv, page_table, p):
    B, T, _ = x.shape
    n_pages = page_table.shape[1]
    past_len = n_pages * PAGE_SIZE
    pos = past_len + jnp.arange(T, dtype=jnp.int32)
    f = front(x, c, pos, p)
    u_out, conv_state = conv_branch(f['u'], state_conv, p)
    n_new = -(-T // BLOCK)
    new_pad = ((0, 0), (0, n_new * BLOCK - T), (0, 0), (0, 0))
    past_k = cache_cmp_k[page_table].reshape(B, past_len, N_KV_HEADS, HEAD_DIM)
    past_v = cache_cmp_v[page_table].reshape(B, past_len, N_KV_HEADS, HEAD_DIM)
    kc = jnp.concatenate([compress(past_k, p['cmp_mod_k'], p['cmp_w_k']),
                          compress(jnp.pad(f['k_cmp'], new_pad), p['cmp_mod_k'], p['cmp_w_k'])], axis=1)
    vc = jnp.concatenate([compress(past_v, p['cmp_mod_v'], p['cmp_w_v']),
                          compress(jnp.pad(f['v_cmp'], new_pad), p['cmp_mod_v'], p['cmp_w_v'])], axis=1)
    bpp = PAGE_SIZE // BLOCK
    n_pool_blocks = cache_sel_k.shape[0] * bpp
    k_pool = jnp.concatenate([cache_sel_k.reshape(n_pool_blocks, BLOCK, N_KV_HEADS, HEAD_DIM),
                              jnp.pad(f['k_sel'], new_pad).reshape(B * n_new, BLOCK, N_KV_HEADS, HEAD_DIM)], axis=0)
    v_pool = jnp.concatenate([cache_sel_v.reshape(n_pool_blocks, BLOCK, N_KV_HEADS, HEAD_DIM),
                              jnp.pad(f['v_sel'], new_pad).reshape(B * n_new, BLOCK, N_KV_HEADS, HEAD_DIM)], axis=0)
    past_tbl = (page_table[:, :, None] * bpp + jnp.arange(bpp, dtype=jnp.int32)).reshape(B, n_pages * bpp)
    new_tbl = n_pool_blocks + jnp.arange(B * n_new, dtype=jnp.int32).reshape(B, n_new)
    block_table = jnp.concatenate([past_tbl, new_tbl], axis=1)
    buf = cache_win_k.shape[1]
    kw = jnp.concatenate([cache_win_k.astype(x.dtype), f['k_win']], axis=1)
    vw = jnp.concatenate([cache_win_v.astype(x.dtype), f['v_win']], axis=1)
    kwpos = past_len - buf + jnp.arange(buf + T, dtype=jnp.int32)
    o = nsa_attend(f['q_plain'], f['q_rope'], pos, kc, vc, k_pool, v_pool, block_table,
                   kw, vw, kwpos, f['branch_g'])
    y = back(x, f, u_out, o, p)
    keep = min(WINDOW, buf + T)
    return y, (f['k_cmp'], f['v_cmp'], f['k_sel'], f['v_sel'], kw[:, -keep:], vw[:, -keep:], conv_state)


def setup_inputs(seed: int = 0) -> dict:
    key = jax.random.key(seed)
    ks = iter(jax.random.split(key, 48))
    nrm = lambda shape, s: s * jax.random.normal(next(ks), shape, jnp.float32)
    n_pages = PAST_LEN // PAGE_SIZE
    used = DEC_BATCH * n_pages
    n_pool = used + max(1, used // 4)
    win_buf = min(WINDOW, PAST_LEN)
    L = DEPTH
    kv_pool = (L, n_pool, PAGE_SIZE, N_KV_HEADS, HEAD_DIM)
    kv_win = (L, DEC_BATCH, win_buf, N_KV_HEADS, HEAD_DIM)
    inputs = {}
    inputs['x_prompt'] = nrm((BATCH, SEQ, D_MODEL), 1.0)
    inputs['x_sample'] = nrm((DEC_BATCH, DEC_SEQ, D_MODEL), 1.0)
    inputs['cache_cmp_k'] = nrm(kv_pool, 1.0)
    inputs['cache_cmp_v'] = nrm(kv_pool, 1.0)
    inputs['cache_sel_k'] = nrm(kv_pool, 1.0)
    inputs['cache_sel_v'] = nrm(kv_pool, 1.0)
    inputs['cache_win_k'] = nrm(kv_win, 1.0)
    inputs['cache_win_v'] = nrm(kv_win, 1.0)
    inputs['state_conv'] = nrm((L, DEC_BATCH, CONV_K - 1, D_CONV), 0.5)
    inputs['page_table'] = jax.random.permutation(next(ks), n_pool)[:used].reshape(DEC_BATCH, n_pages).astype(jnp.int32)
    inputs['c_prompt'] = nrm((BATCH, D_MODEL), 1.0)
    inputs['c_sample'] = nrm((DEC_BATCH, D_MODEL), 1.0)
    inputs['w_ada'] = nrm((L, D_MODEL, 6 * D_MODEL), 0.5 * D_MODEL ** -0.5)
    inputs['b_ada'] = nrm((L, 6 * D_MODEL), 0.01)
    inputs['mix_norm_g'] = 1.0 + nrm((L, D_MODEL), 0.01)
    inputs['w_in'] = nrm((L, D_MODEL, D_IN), D_MODEL ** -0.5)
    inputs['q_norm_g'] = 1.0 + nrm((L, HEAD_DIM), 0.01)
    inputs['k_norm_g'] = 1.0 + nrm((L, 3, HEAD_DIM), 0.01)
    inputs['w_dw'] = nrm((L, CONV_K, D_CONV), CONV_K ** -0.5)
    inputs['b_dw'] = nrm((L, D_CONV), 0.01)
    inputs['conv_ln_g'] = 1.0 + nrm((L, D_CONV), 0.01)
    inputs['conv_ln_b'] = nrm((L, D_CONV), 0.01)
    inputs['w_pw2'] = nrm((L, D_CONV, D_MODEL), D_CONV ** -0.5)
    inputs['cmp_mod_k'] = nrm((L, BLOCK, HEAD_DIM), 0.1)
    inputs['cmp_w_k'] = nrm((L, HEAD_DIM, HEAD_DIM), HEAD_DIM ** -0.5)
    inputs['cmp_mod_v'] = nrm((L, BLOCK, HEAD_DIM), 0.1)
    inputs['cmp_w_v'] = nrm((L, HEAD_DIM, HEAD_DIM), HEAD_DIM ** -0.5)
    inputs['w_nsa_o'] = nrm((L, Q_W, D_MODEL), Q_W ** -0.5)
    inputs['w_out'] = nrm((L, D_MODEL, D_MODEL), D_MODEL ** -0.5)
    inputs['ffn_norm_g'] = 1.0 + nrm((L, D_MODEL), 0.01)
    inputs['w_gate'] = nrm((L, D_MODEL, D_FF), D_MODEL ** -0.5)
    inputs['w_up'] = nrm((L, D_MODEL, D_FF), D_MODEL ** -0.5)
    inputs['w_down'] = nrm((L, D_FF, D_MODEL), D_FF ** -0.5)
    return inputs


def reference(x_prompt, x_sample, cache_cmp_k, cache_cmp_v, cache_sel_k, cache_sel_v,
              cache_win_k, cache_win_v, state_conv, page_table, c_prompt, c_sample,
              w_ada, b_ada, mix_norm_g, w_in, q_norm_g, k_norm_g, w_dw, b_dw, conv_ln_g,
              conv_ln_b, w_pw2, cmp_mod_k, cmp_w_k, cmp_mod_v, cmp_w_v, w_nsa_o, w_out,
              ffn_norm_g, w_gate, w_up, w_down):
    yp, ys = x_prompt, x_sample
    st_p, st_s = [], []
    for l in range(DEPTH):
        p = dict(w_ada=w_ada[l], b_ada=b_ada[l], mix_norm_g=mix_norm_g[l], w_in=w_in[l],
                 q_norm_g=q_norm_g[l], k_norm_g=k_norm_g[l], w_dw=w_dw[l], b_dw=b_dw[l],
                 conv_ln_g=conv_ln_g[l], conv_ln_b=conv_ln_b[l], w_pw2=w_pw2[l],
                 cmp_mod_k=cmp_mod_k[l], cmp_w_k=cmp_w_k[l], cmp_mod_v=cmp_mod_v[l],
                 cmp_w_v=cmp_w_v[l], w_nsa_o=w_nsa_o[l], w_out=w_out[l],
                 ffn_norm_g=ffn_norm_g[l], w_gate=w_gate[l], w_up=w_up[l], w_down=w_down[l])
        yp, sp = prompt_layer(yp, c_prompt, p)
        ys, ss = sample_layer(ys, c_sample, cache_cmp_k[l], cache_cmp_v[l], cache_sel_k[l],
                              cache_sel_v[l], cache_win_k[l], cache_win_v[l], state_conv[l],
                              page_table, p)
        st_p.append(sp)
        st_s.append(ss)
    stk = lambda lst, j: jnp.stack([s[j] for s in lst])
    return (yp, ys,
            stk(st_p, 0), stk(st_p, 1), stk(st_p, 2), stk(st_p, 3), stk(st_p, 4), stk(st_p, 5), stk(st_p, 6),
            stk(st_s, 0), stk(st_s, 1), stk(st_s, 2), stk(st_s, 3), stk(st_s, 4), stk(st_s, 5), stk(st_s, 6))
```

```python
import functools

import jax
import jax.numpy as jnp
from jax import lax
from jax.experimental import pallas as pl
from jax.experimental.pallas import tpu as pltpu

F32 = jnp.float32
BF16 = jnp.bfloat16

D_MODEL = 1024
D_CONV = 512
CONV_K = 31
N_HEADS = 8
N_KV = 2
HEAD_DIM = 64
GROUP = 4
Q_W = N_HEADS * HEAD_DIM
KV_W = N_KV * HEAD_DIM
BLOCK = 64
N_SEL = 16
WINDOW = 512
ROPE_THETA = 10000.0
D_FF = 2816
EPS = 1e-6
FORCED_BONUS = 2.0 * GROUP
PAGE = 128
OFF_Q = 2 * D_CONV
OFF_KV = OFF_Q + Q_W
OFF_NG = OFF_KV + 6 * KV_W
OFF_MG = OFF_NG + 3 * N_HEADS
SCALE = HEAD_DIM ** -0.5
NEG = -1e30
N_T_ROWS = 1312
CONV_HALO = 32
SEL_CHUNK = 512
WIN_CHUNK = 128
VMEM_LIMIT = 56 * 1024 * 1024


def _cparams(sem):
    return pltpu.CompilerParams(dimension_semantics=sem, vmem_limit_bytes=VMEM_LIMIT)


def _ada_kernel(c_ref, w_ref, b_ref, o_ref):
    c = c_ref[...]
    s = c * jax.nn.sigmoid(c)
    o_ref[...] = jnp.dot(s.astype(BF16), w_ref[...].astype(BF16), preferred_element_type=F32) + b_ref[...]


def _ada(c, w_ada, b_ada):
    n, tn = c.shape[0], 1024
    return pl.pallas_call(
        _ada_kernel,
        grid=(w_ada.shape[1] // tn,),
        in_specs=[pl.BlockSpec((n, D_MODEL), lambda j: (0, 0)),
                  pl.BlockSpec((D_MODEL, tn), lambda j: (0, j)),
                  pl.BlockSpec((1, tn), lambda j: (0, j))],
        out_specs=pl.BlockSpec((n, tn), lambda j: (0, j)),
        out_shape=jax.ShapeDtypeStruct((n, w_ada.shape[1]), F32),
        compiler_params=_cparams(("arbitrary",)),
        name="ada",
    )(c, w_ada, b_ada.reshape(1, -1))


def _front_kernel(x_ref, sh1_ref, sc1_ref, g_ref, wn_ref, wt_ref, gq_ref, gk_ref, cos_ref, sin_ref,
                  u_ref, gmg_ref, qp_ref, qr_ref, kcmp_ref, vcmp_ref, ksel_ref, vsel_ref, kwin_ref, vwin_ref,
                  kselb_ref, kwinb_ref, vselT_ref, vwinT_ref, ngT_ref):
    x = x_ref[0]
    ms = jnp.mean(x * x, axis=-1, keepdims=True)
    h = x * lax.rsqrt(ms + EPS) * g_ref[...]
    h = h * (1.0 + sc1_ref[0]) + sh1_ref[0]
    hb = h.astype(BF16)
    zn = jnp.dot(hb, wn_ref[...], preferred_element_type=F32)
    u_ref[0] = zn[:, :D_CONV] * jax.nn.sigmoid(zn[:, D_CONV:2 * D_CONV])
    gmg_ref[0] = jax.nn.sigmoid(zn[:, 2 * D_CONV:])
    zt = lax.dot_general(wt_ref[...], hb, (((1,), (1,)), ((), ())), preferred_element_type=F32)
    cos = cos_ref[...]
    sin = sin_ref[...]
    half = HEAD_DIM // 2

    def norm(v, g):
        m = jnp.mean(v * v, axis=0, keepdims=True)
        return v * lax.rsqrt(m + EPS) * g

    def rope(v):
        a, b = v[:half], v[half:]
        return jnp.concatenate([a * cos - b * sin, b * cos + a * sin], axis=0)

    gq = gq_ref[...]
    for hh in range(N_HEADS):
        q = norm(zt[hh * HEAD_DIM:(hh + 1) * HEAD_DIM], gq) * SCALE
        qp_ref[0, hh * HEAD_DIM:(hh + 1) * HEAD_DIM, :] = q.astype(BF16)
        qr_ref[0, hh * HEAD_DIM:(hh + 1) * HEAD_DIM, :] = rope(q).astype(BF16)

    def head_pair(j, g=None, rotary=False):
        outs = []
        for hh in range(N_KV):
            lo = Q_W + j * KV_W + hh * HEAD_DIM
            v = zt[lo:lo + HEAD_DIM]
            if g is not None:
                v = norm(v, g)
            if rotary:
                v = rope(v)
            outs.append(v)
        return jnp.concatenate(outs, axis=0)

    k_cmp = head_pair(0, gk_ref[0])
    v_cmp = head_pair(1)
    k_sel = head_pair(2, gk_ref[1], True)
    v_sel = head_pair(3)
    k_win = head_pair(4, gk_ref[2], True)
    v_win = head_pair(5)
    kcmp_ref[0] = k_cmp.T
    vcmp_ref[0] = v_cmp.T
    k_sel_n = k_sel.T
    k_win_n = k_win.T
    ksel_ref[0] = k_sel_n
    vsel_ref[0] = v_sel.T
    kwin_ref[0] = k_win_n
    vwin_ref[0] = v_win.T
    kselb_ref[0] = k_sel_n.astype(BF16)
    kwinb_ref[0] = k_win_n.astype(BF16)
    vselT_ref[0] = v_sel.astype(BF16)
    vwinT_ref[0] = v_win.astype(BF16)
    ngT_ref[0] = jax.nn.sigmoid(zt[Q_W + 6 * KV_W:])


def _front(x, mod, pos, prm, tm):
    B, T, _ = x.shape
    Tm = mod.shape[1]
    tmm = 1 if Tm == 1 else tm
    half = HEAD_DIM // 2
    inv = jnp.power(ROPE_THETA, -jnp.arange(half, dtype=F32) / half)
    ang = pos.astype(F32)[:, None] * inv[None, :]
    cosT, sinT = jnp.cos(ang).T, jnp.sin(ang).T
    gq = jnp.broadcast_to(prm["q_norm_g"][:, None], (HEAD_DIM, tm))
    gk = jnp.broadcast_to(prm["k_norm_g"][:, :, None], (3, HEAD_DIM, tm))
    nt = T // tm
    tok = lambda w: pl.BlockSpec((1, tm, w), lambda b, t: (b, t, 0))
    tokT = lambda r: pl.BlockSpec((1, r, tm), lambda b, t: (b, 0, t))
    modspec = lambda k: pl.BlockSpec((1, tmm, D_MODEL), (lambda b, t: (b, t, k)) if Tm != 1 else (lambda b, t: (b, 0, k)))
    const = lambda shape: pl.BlockSpec(shape, lambda b, t: (0,) * len(shape))
    sd = jax.ShapeDtypeStruct
    outs = pl.pallas_call(
        _front_kernel,
        grid=(B, nt),
        in_specs=[tok(D_MODEL), modspec(0), modspec(1), const((1, D_MODEL)),
                  const((D_MODEL, 3 * D_MODEL)), const((N_T_ROWS, D_MODEL)),
                  const((HEAD_DIM, tm)), const((3, HEAD_DIM, tm)),
                  pl.BlockSpec((half, tm), lambda b, t: (0, t)), pl.BlockSpec((half, tm), lambda b, t: (0, t))],
        out_specs=[tok(D_CONV), tok(2 * D_MODEL), tokT(Q_W), tokT(Q_W)] + [tok(KV_W)] * 6
                  + [tok(KV_W), tok(KV_W), tokT(KV_W), tokT(KV_W), tokT(32)],
        out_shape=[sd((B, T, D_CONV), F32), sd((B, T, 2 * D_MODEL), F32), sd((B, Q_W, T), BF16), sd((B, Q_W, T), BF16)]
                  + [sd((B, T, KV_W), F32)] * 6
                  + [sd((B, T, KV_W), BF16), sd((B, T, KV_W), BF16), sd((B, KV_W, T), BF16), sd((B, KV_W, T), BF16),
                     sd((B, 32, T), F32)],
        compiler_params=_cparams(("parallel", "arbitrary")),
        name="front",
    )(x, mod, mod, prm["mix_norm_g"].reshape(1, -1), prm["wn"], prm["wt"], gq, gk, cosT, sinT)
    names = ("u", "gmg", "qp", "qr", "k_cmp", "v_cmp", "k_sel", "v_sel", "k_win", "v_win",
             "kselb", "kwinb", "vselT", "vwinT", "ngT")
    return dict(zip(names, outs))


def _conv_kernel(u_ref, st_ref, wdw_ref, bdw_ref, lng_ref, lnb_ref, wpw_ref, uo_ref, so_ref, ext_ref, *, tm):
    t = pl.program_id(1)
    pad = CONV_HALO - (CONV_K - 1)

    @pl.when(t == 0)
    def _():
        ext_ref[0:pad, :] = jnp.zeros((pad, D_CONV), F32)
        ext_ref[pad:CONV_HALO, :] = st_ref[0]

    ext_ref[CONV_HALO:CONV_HALO + tm, :] = u_ref[0]
    y = jnp.zeros((tm, D_CONV), F32) + bdw_ref[...]
    for k in range(CONV_K):
        y = y + ext_ref[pad + k:pad + k + tm, :] * wdw_ref[k:k + 1, :]
    mu = jnp.mean(y, axis=-1, keepdims=True)
    yc = y - mu
    var = jnp.mean(yc * yc, axis=-1, keepdims=True)
    yn = yc * lax.rsqrt(var + EPS) * lng_ref[...] + lnb_ref[...]
    ya = yn * jax.nn.sigmoid(yn)
    uo_ref[0] = jnp.dot(ya.astype(BF16), wpw_ref[...], preferred_element_type=F32)
    so_ref[0] = ext_ref[tm + pad:tm + CONV_HALO, :]
    carry = ext_ref[tm:tm + CONV_HALO, :]
    ext_ref[0:CONV_HALO, :] = carry


def _conv(u, state, prm, tm):
    B, T, _ = u.shape
    const = lambda shape: pl.BlockSpec(shape, lambda b, t: (0,) * len(shape))
    return pl.pallas_call(
        functools.partial(_conv_kernel, tm=tm),
        grid=(B, T // tm),
        in_specs=[pl.BlockSpec((1, tm, D_CONV), lambda b, t: (b, t, 0)),
                  pl.BlockSpec((1, CONV_K - 1, D_CONV), lambda b, t: (b, 0, 0)),
                  const((CONV_K, D_CONV)), const((1, D_CONV)), const((1, D_CONV)), const((1, D_CONV)),
                  const((D_CONV, D_MODEL))],
        out_specs=[pl.BlockSpec((1, tm, D_MODEL), lambda b, t: (b, t, 0)),
                   pl.BlockSpec((1, CONV_K - 1, D_CONV), lambda b, t: (b, 0, 0))],
        out_shape=[jax.ShapeDtypeStruct((B, T, D_MODEL), F32), jax.ShapeDtypeStruct((B, CONV_K - 1, D_CONV), F32)],
        scratch_shapes=[pltpu.VMEM((CONV_HALO + tm, D_CONV), F32)],
        compiler_params=_cparams(("parallel", "arbitrary")),
        name="conv",
    )(u, state, prm["w_dw"], prm["b_dw"].reshape(1, -1), prm["conv_ln_g"].reshape(1, -1),
      prm["conv_ln_b"].reshape(1, -1), prm["w_pw2b"])


def _compress_kernel(*refs, n_in, n_prefetch=0):
    refs = refs[n_prefetch:]
    k_refs, v_refs = refs[:n_in], refs[n_in:2 * n_in]
    modk_ref, wk_ref, modv_ref, wv_ref, kc_ref, vc_ref = refs[2 * n_in:]

    def one(rows_refs, mod_ref, w_ref, out_ref):
        rows = jnp.concatenate([r[0] for r in rows_refs], axis=0) if n_in > 1 else rows_refs[0][0]
        nb = rows.shape[0] // BLOCK
        blocks = rows.reshape(nb, BLOCK, KV_W) * (1.0 + mod_ref[...])[None]
        summ = jnp.sum(blocks, axis=1) * (1.0 / BLOCK)
        out_ref[0] = jnp.dot(summ.astype(BF16), w_ref[...], preferred_element_type=F32)

    one(k_refs, modk_ref, wk_ref, kc_ref)
    one(v_refs, modv_ref, wv_ref, vc_ref)


def _compress_consts(prm):
    tile2 = lambda m: jnp.tile(m, (1, N_KV))
    bd = lambda w: jnp.kron(jnp.eye(N_KV, dtype=F32), w).astype(BF16)
    return tile2(prm["cmp_mod_k"]), bd(prm["cmp_w_k"]), tile2(prm["cmp_mod_v"]), bd(prm["cmp_w_v"])


def _compress(rows_k, rows_v, prm, rows_per_step):
    B, L, _ = rows_k.shape
    nb = rows_per_step // BLOCK
    const = lambda shape: pl.BlockSpec(shape, lambda b, j: (0,) * len(shape))
    rspec = pl.BlockSpec((1, rows_per_step, KV_W), lambda b, j: (b, j, 0))
    ospec = pl.BlockSpec((1, nb, KV_W), lambda b, j: (b, j, 0))
    osh = jax.ShapeDtypeStruct((B, L // BLOCK, KV_W), F32)
    return pl.pallas_call(
        functools.partial(_compress_kernel, n_in=1),
        grid=(B, L // rows_per_step),
        in_specs=[rspec, rspec, const((BLOCK, KV_W)), const((KV_W, KV_W)), const((BLOCK, KV_W)), const((KV_W, KV_W))],
        out_specs=[ospec, ospec],
        out_shape=[osh, osh],
        compiler_params=_cparams(("parallel", "arbitrary")),
        name="compress",
    )(rows_k, rows_v, *_compress_consts(prm))


def _compress_paged(cache_k, cache_v, page_table, prm, pages_per_step):
    B, n_pages = page_table.shape
    nb = pages_per_step * PAGE // BLOCK
    const = lambda shape: pl.BlockSpec(shape, lambda b, j, pt: (0,) * len(shape))
    pspec = lambda i: pl.BlockSpec((1, PAGE, KV_W), lambda b, j, pt: (pt[b * n_pages + j * pages_per_step + i], 0, 0))
    ospec = pl.BlockSpec((1, nb, KV_W), lambda b, j, pt: (b, j, 0))
    osh = jax.ShapeDtypeStruct((B, n_pages * PAGE // BLOCK, KV_W), F32)
    grid_spec = pltpu.PrefetchScalarGridSpec(
        num_scalar_prefetch=1,
        grid=(B, n_pages // pages_per_step),
        in_specs=[pspec(i) for i in range(pages_per_step)] * 2
                 + [const((BLOCK, KV_W)), const((KV_W, KV_W)), const((BLOCK, KV_W)), const((KV_W, KV_W))],
        out_specs=[ospec, ospec],
    )
    return pl.pallas_call(
        functools.partial(_compress_kernel, n_in=pages_per_step, n_prefetch=1),
        grid_spec=grid_spec,
        out_shape=[osh, osh],
        compiler_params=_cparams(("parallel", "arbitrary")),
        name="compress_paged",
    )(page_table.reshape(-1), *([cache_k] * pages_per_step), *([cache_v] * pages_per_step), *_compress_consts(prm))


def _attend(qT, k, vT, mask, carry):
    m, l, acc = carry
    s = jnp.dot(k, qT, preferred_element_type=F32)
    s = jnp.where(mask, s, NEG)
    m_new = jnp.maximum(m, jnp.max(s, axis=0, keepdims=True))
    alpha = jnp.exp(m - m_new)
    p = jnp.exp(s - m_new)
    l = alpha * l + jnp.sum(p, axis=0, keepdims=True)
    acc = alpha * acc + jnp.dot(vT, p.astype(BF16), preferred_element_type=F32)
    return m_new, l, acc


def _attend_init(L):
    return jnp.full((1, L), NEG, F32), jnp.zeros((1, L), F32), jnp.zeros((HEAD_DIM, L), F32)


def _attend_out(carry):
    _, l, acc = carry
    return acc / jnp.maximum(l, 1e-30)


def _block_mask(sel_rows, kc):
    nb, L = sel_rows.shape
    return jnp.broadcast_to((sel_rows > 0.5)[:, None, :], (nb, BLOCK, L)).reshape(kc, L)


def _cmp_and_select(qpT, kc, vcT, qpos, tq):
    nb, L = kc.shape[0], qpT.shape[1]
    s = jnp.dot(kc, qpT, preferred_element_type=F32)
    blk = lax.broadcasted_iota(jnp.int32, (nb, L), 0)
    ok = (blk + 1) * BLOCK <= qpos + 1
    s = jnp.where(ok, s, -jnp.inf)
    m = jnp.max(s, axis=0, keepdims=True)
    m = jnp.where(m > -jnp.inf, m, 0.0)
    e = jnp.where(ok, jnp.exp(s - m), 0.0)
    p = e / jnp.maximum(jnp.sum(e, axis=0, keepdims=True), 1e-30)
    o_cmp = jnp.dot(vcT, p.astype(BF16), preferred_element_type=F32)
    imp = p
    for g in range(1, GROUP):
        imp = imp + pltpu.roll(p, g * tq, 1)
    cur = lax.shift_right_logical(qpos, 6)
    visible = blk <= cur
    forced = (blk == 0) | (blk == cur) | (blk == cur - 1)
    score = jnp.where(visible, imp + jnp.where(forced, FORCED_BONUS, 0.0), -1.0)
    if tq % 128 == 0:
        score = score[:, :tq]
    ridx = lax.broadcasted_iota(jnp.int32, score.shape, 0).astype(F32)
    sel = jnp.zeros_like(score)
    rem = score
    for _ in range(min(N_SEL, nb)):
        mx = jnp.max(rem, axis=0, keepdims=True)
        first = jnp.min(jnp.where(rem == mx, ridx, float(nb)), axis=0, keepdims=True)
        pick = ridx == first
        rem = jnp.where(pick, -jnp.inf, rem)
        sel = jnp.where(pick, 1.0, sel)
    if tq % 128 == 0:
        sel = jnp.concatenate([sel] * GROUP, axis=1)
    return o_cmp, sel


def _nsa_prompt_kernel(qp_ref, qr_ref, kc_ref, vcT_ref, ksel_ref, vselT_ref, kwin_ref, vwinT_ref, ngT_ref,
                       o_ref, sel_ref, *, tq):
    t = pl.program_id(1)
    s0 = t * tq
    L = GROUP * tq
    lane = lax.broadcasted_iota(jnp.int32, (1, L), 1)
    qpos = s0 + lane % tq
    ng = ngT_ref[0]

    for h in range(N_KV):
        def qpad(ref):
            q = jnp.concatenate([ref[0, (h * GROUP + g) * HEAD_DIM:(h * GROUP + g + 1) * HEAD_DIM, :]
                                 for g in range(GROUP)], axis=1)
            z = jnp.zeros_like(q)
            return jnp.concatenate([q, z] if h == 0 else [z, q], axis=0)

        qp, qr = qpad(qp_ref), qpad(qr_ref)
        hs = slice(h * HEAD_DIM, (h + 1) * HEAD_DIM)
        o_cmp, sel = _cmp_and_select(qp, kc_ref[0], vcT_ref[0, hs, :], qpos, tq)
        sel_ref[...] = sel

        def sel_step(c, carry, causal):
            k0 = pl.multiple_of(c * SEL_CHUNK, SEL_CHUNK)
            r0 = pl.multiple_of(c * (SEL_CHUNK // BLOCK), SEL_CHUNK // BLOCK)
            mask = _block_mask(sel_ref[pl.ds(r0, SEL_CHUNK // BLOCK), :], SEL_CHUNK)
            if causal:
                kpos = k0 + lax.broadcasted_iota(jnp.int32, (SEL_CHUNK, L), 0)
                mask = mask & (kpos <= qpos)
            return _attend(qr, ksel_ref[0, pl.ds(k0, SEL_CHUNK), :], vselT_ref[0, hs, pl.ds(k0, SEL_CHUNK)], mask, carry)

        n_full = s0 // SEL_CHUNK
        carry = lax.fori_loop(0, n_full, lambda c, cr: sel_step(c, cr, False), _attend_init(L))
        o_sel = _attend_out(sel_step(n_full, carry, True))

        def win_step(c, carry):
            k0 = pl.multiple_of(c * WIN_CHUNK, WIN_CHUNK)
            kpos = k0 + lax.broadcasted_iota(jnp.int32, (WIN_CHUNK, L), 0)
            mask = (kpos <= qpos) & (kpos > qpos - WINDOW)
            return _attend(qr, kwin_ref[0, pl.ds(k0, WIN_CHUNK), :], vwinT_ref[0, hs, pl.ds(k0, WIN_CHUNK)], mask, carry)

        c_lo = jnp.maximum(s0 - WINDOW, 0) // WIN_CHUNK
        c_hi = (s0 + tq) // WIN_CHUNK
        o_win = _attend_out(lax.fori_loop(c_lo, c_hi, win_step, _attend_init(L)))

        def gate(br):
            return jnp.concatenate([ng[h * 12 + g * 3 + br:h * 12 + g * 3 + br + 1, :] for g in range(GROUP)], axis=1)

        o = gate(0) * o_cmp + gate(1) * o_sel + gate(2) * o_win
        oT = jnp.concatenate([o[:, g * tq:(g + 1) * tq] for g in range(GROUP)], axis=0)
        o_ref[0, :, h * GROUP * HEAD_DIM:(h + 1) * GROUP * HEAD_DIM] = oT.T.astype(o_ref.dtype)


def _nsa_prompt(f, kcb, vcTb, tq):
    B, _, T = f["qp"].shape
    nb = T // BLOCK
    full = lambda shape: pl.BlockSpec((1,) + shape, lambda b, t: (b, 0, 0))
    return pl.pallas_call(
        functools.partial(_nsa_prompt_kernel, tq=tq),
        grid=(B, T // tq),
        in_specs=[pl.BlockSpec((1, Q_W, tq), lambda b, t: (b, 0, t)), pl.BlockSpec((1, Q_W, tq), lambda b, t: (b, 0, t)),
                  full((nb, KV_W)), full((KV_W, nb)),
                  full((T, KV_W)), full((KV_W, T)), full((T, KV_W)), full((KV_W, T)),
                  pl.BlockSpec((1, 32, tq), lambda b, t: (b, 0, t))],
        out_specs=pl.BlockSpec((1, tq, Q_W), lambda b, t: (b, t, 0)),
        out_shape=jax.ShapeDtypeStruct((B, T, Q_W), BF16),
        scratch_shapes=[pltpu.VMEM((nb, GROUP * tq), F32)],
        compiler_params=_cparams(("parallel", "arbitrary")),
        name="nsa_prompt",
    )(f["qp"], f["qr"], kcb, vcTb, f["kselb"], f["vselT"], f["kwinb"], f["vwinT"], f["ngT"])


def _nsa_sample_kernel(pt_ref, qp_ref, qr_ref, kc_ref, vcT_ref, gate_ref, *refs, tqs, past_len, pages_per_step):
    del pt_ref
    kpages, vpages = refs[:pages_per_step], refs[pages_per_step:2 * pages_per_step]
    (knew_ref, vnewT_ref, cwk_ref, cwv_ref, kwnew_ref, vwnewT_ref,
     o_ref, sel_ref, ocmp_ref, m_ref, l_ref, acc_ref) = refs[2 * pages_per_step:]
    j = pl.program_id(1)
    n_steps = pl.num_programs(1)
    L = GROUP * tqs
    kc_step = pages_per_step * PAGE
    lane = lax.broadcasted_iota(jnp.int32, (1, L), 1)
    qi = lane % tqs
    qpos = past_len + qi
    heads = [slice(h * HEAD_DIM, (h + 1) * HEAD_DIM) for h in range(N_KV)]

    @pl.when(j == 0)
    def _():
        for h in range(N_KV):
            o_cmp, sel = _cmp_and_select(qp_ref[0, h], kc_ref[0], vcT_ref[0, heads[h], :], qpos, tqs)
            sel_ref[h] = sel
            ocmp_ref[h] = o_cmp
            m, l, acc = _attend_init(L)
            m_ref[h], l_ref[h], acc_ref[h] = m, l, acc

    k = jnp.concatenate([r[0] for r in kpages], axis=0).astype(BF16)
    vT = jnp.concatenate([r[0] for r in vpages], axis=0).T.astype(BF16)
    r0 = pl.multiple_of(j * (kc_step // BLOCK), kc_step // BLOCK)
    for h in range(N_KV):
        mask = _block_mask(sel_ref[h, pl.ds(r0, kc_step // BLOCK), :], kc_step)
        m, l, acc = _attend(qr_ref[0, h], k, vT[heads[h]], mask, (m_ref[h], l_ref[h], acc_ref[h]))
        m_ref[h], l_ref[h], acc_ref[h] = m, l, acc

    @pl.when(j == n_steps - 1)
    def _():
        nb_past = past_len // BLOCK
        buf = cwk_ref.shape[1]
        r_new = lax.broadcasted_iota(jnp.int32, (tqs, L), 0)
        r_buf = lax.broadcasted_iota(jnp.int32, (buf, L), 0)
        cwk = cwk_ref[0].astype(BF16)
        cwvT = cwv_ref[0].T.astype(BF16)
        gates = gate_ref[0]
        for h in range(N_KV):
            qr = qr_ref[0, h]
            mask = (sel_ref[h, nb_past:nb_past + 1, :] > 0.5) & (r_new <= qi)
            carry = _attend(qr, knew_ref[0], vnewT_ref[0, heads[h], :], mask, (m_ref[h], l_ref[h], acc_ref[h]))
            o_sel = _attend_out(carry)
            carry = _attend(qr, cwk, cwvT[heads[h]], r_buf > qi + (buf - WINDOW), _attend_init(L))
            carry = _attend(qr, kwnew_ref[0], vwnewT_ref[0, heads[h], :], r_new <= qi, carry)
            o_win = _attend_out(carry)
            g = gates[h]
            o_ref[0, h] = g[0:1] * ocmp_ref[h] + g[1:2] * o_sel + g[2:3] * o_win


def _nsa_sample(qpT, qrT, kcb, vcTb, gates, cache_sel_k, cache_sel_v, page_table, knew, vnewT,
                cache_win_k, cache_win_v, kwnew, vwnewT, tqs, pages_per_step):
    Bs, n_pages = page_table.shape
    L = GROUP * tqs
    nbp = kcb.shape[1]
    buf = cache_win_k.shape[1]
    per_b = lambda shape: pl.BlockSpec((1,) + shape, lambda b, j, pt: (b,) + (0,) * len(shape))
    pspec = lambda i: pl.BlockSpec((1, PAGE, KV_W), lambda b, j, pt: (pt[b * n_pages + j * pages_per_step + i], 0, 0))
    grid_spec = pltpu.PrefetchScalarGridSpec(
        num_scalar_prefetch=1,
        grid=(Bs, n_pages // pages_per_step),
        in_specs=[per_b((N_KV, KV_W, L)), per_b((N_KV, KV_W, L)), per_b((nbp, KV_W)), per_b((KV_W, nbp)),
                  per_b((N_KV, 8, L))]
                 + [pspec(i) for i in range(pages_per_step)] * 2
                 + [per_b((tqs, KV_W)), per_b((KV_W, tqs)), per_b((buf, KV_W)), per_b((buf, KV_W)),
                    per_b((tqs, KV_W)), per_b((KV_W, tqs))],
        out_specs=per_b((N_KV, HEAD_DIM, L)),
        scratch_shapes=[pltpu.VMEM((N_KV, nbp, L), F32), pltpu.VMEM((N_KV, HEAD_DIM, L), F32),
                        pltpu.VMEM((N_KV, 1, L), F32), pltpu.VMEM((N_KV, 1, L), F32),
                        pltpu.VMEM((N_KV, HEAD_DIM, L), F32)],
    )
    return pl.pallas_call(
        functools.partial(_nsa_sample_kernel, tqs=tqs, past_len=n_pages * PAGE, pages_per_step=pages_per_step),
        grid_spec=grid_spec,
        out_shape=jax.ShapeDtypeStruct((Bs, N_KV, HEAD_DIM, L), F32),
        compiler_params=_cparams(("parallel", "arbitrary")),
        name="nsa_sample",
    )(page_table.reshape(-1), qpT, qrT, kcb, vcTb, gates,
      *([cache_sel_k] * pages_per_step), *([cache_sel_v] * pages_per_step),
      knew, vnewT, cache_win_k, cache_win_v, kwnew, vwnewT)


def _back_kernel(x_ref, o_ref, uo_ref, gmg_ref, g1_ref, sh2_ref, sc2_ref, g2_ref, fg_ref,
                 wno_ref, wout_ref, wg_ref, wu_ref, wd_ref, y_ref):
    nsa = jnp.dot(o_ref[0], wno_ref[...], preferred_element_type=F32)
    gmg = gmg_ref[0]
    merged = gmg[:, :D_MODEL] * uo_ref[0] + gmg[:, D_MODEL:] * nsa
    mix = jnp.dot(merged.astype(BF16), wout_ref[...], preferred_element_type=F32)
    x1 = x_ref[0] + g1_ref[0] * mix
    ms = jnp.mean(x1 * x1, axis=-1, keepdims=True)
    h = x1 * lax.rsqrt(ms + EPS) * fg_ref[...]
    h = h * (1.0 + sc2_ref[0]) + sh2_ref[0]
    hb = h.astype(BF16)
    a = jnp.dot(hb, wg_ref[...], preferred_element_type=F32)
    b = jnp.dot(hb, wu_ref[...], preferred_element_type=F32)
    act = a * jax.nn.sigmoid(a) * b
    ffn = jnp.dot(act.astype(BF16), wd_ref[...], preferred_element_type=F32)
    y_ref[0] = x1 + g2_ref[0] * ffn


def _back(x, o, u_out, gmg, mod, prm, tm):
    B, T, _ = x.shape
    Tm = mod.shape[1]
    tmm = 1 if Tm == 1 else tm
    tok = lambda w: pl.BlockSpec((1, tm, w), lambda b, t: (b, t, 0))
    modspec = lambda k: pl.BlockSpec((1, tmm, D_MODEL), (lambda b, t: (b, t, k)) if Tm != 1 else (lambda b, t: (b, 0, k)))
    wspec = lambda shape: pl.BlockSpec(shape, lambda b, t: (0, 0), pipeline_mode=pl.Buffered(1))
    return pl.pallas_call(
        _back_kernel,
        grid=(B, T // tm),
        in_specs=[tok(D_MODEL), tok(Q_W), tok(D_MODEL), tok(2 * D_MODEL),
                  modspec(2), modspec(3), modspec(4), modspec(5), wspec((1, D_MODEL)),
                  wspec((Q_W, D_MODEL)), wspec((D_MODEL, D_MODEL)), wspec((D_MODEL, D_FF)), wspec((D_MODEL, D_FF)),
                  wspec((D_FF, D_MODEL))],
        out_specs=tok(D_MODEL),
        out_shape=jax.ShapeDtypeStruct((B, T, D_MODEL), F32),
        compiler_params=_cparams(("parallel", "arbitrary")),
        name="back",
    )(x, o, u_out, gmg, mod, mod, mod, mod, prm["ffn_norm_g"].reshape(1, -1),
      prm["w_nsa_ob"], prm["w_outb"], prm["w_gateb"], prm["w_upb"], prm["w_downb"])


def _prep_params(p):
    w_in = p["w_in"]
    q = dict(p)
    q["wn"] = jnp.concatenate([w_in[:, :OFF_Q], w_in[:, OFF_MG:]], axis=1).astype(BF16)
    wt = jnp.concatenate([w_in[:, OFF_Q:OFF_MG], jnp.zeros((D_MODEL, N_T_ROWS - (OFF_MG - OFF_Q)), F32)], axis=1)
    q["wt"] = wt.T.astype(BF16)
    q["w_pw2b"] = p["w_pw2"].astype(BF16)
    for name in ("w_nsa_o", "w_out", "w_gate", "w_up", "w_down"):
        q[name + "b"] = p[name].astype(BF16)
    return q


def _heads(a):
    return a.reshape(a.shape[0], a.shape[1], N_KV, HEAD_DIM)


def _prompt_layer(x, mod, prm, tm, tq):
    B, T, _ = x.shape
    mod = mod[:, None, :]
    f = _front(x, mod, jnp.arange(T, dtype=jnp.int32), prm, tm)
    u_out, conv_state = _conv(f["u"], jnp.zeros((B, CONV_K - 1, D_CONV), F32), prm, tm)
    kc, vc = _compress(f["k_cmp"], f["v_cmp"], prm, min(T, 1024))
    o = _nsa_prompt(f, kc.astype(BF16), jnp.swapaxes(vc, 1, 2).astype(BF16), tq)
    y = _back(x, o, u_out, f["gmg"], mod, prm, tm)
    keep = min(WINDOW, T)
    return y, (_heads(f["k_cmp"]), _heads(f["v_cmp"]), _heads(f["k_sel"]), _heads(f["v_sel"]),
               _heads(f["k_win"])[:, -keep:], _heads(f["v_win"])[:, -keep:], conv_state)


def _sample_layer(x, mod, cache_cmp_k, cache_cmp_v, cache_sel_k, cache_sel_v, cache_win_k, cache_win_v,
                  state_conv, page_table, prm):
    Bs, Ts, _ = x.shape
    n_pages = page_table.shape[1]
    past_len = n_pages * PAGE
    n_tok = Bs * Ts
    tqs = 32
    assert n_tok % 128 == 0 and Ts <= tqs and n_pages % 8 == 0
    pool = lambda c: c.reshape(c.shape[0], PAGE, KV_W)
    pos = past_len + jnp.arange(n_tok, dtype=jnp.int32) % Ts
    mod_tok = jnp.repeat(mod, Ts, axis=0)[None]
    f = _front(x.reshape(1, n_tok, D_MODEL), mod_tok, pos, prm, n_tok)
    u_out, conv_state = _conv(f["u"].reshape(Bs, Ts, D_CONV), state_conv, prm, Ts)

    kc_past, vc_past = _compress_paged(pool(cache_cmp_k), pool(cache_cmp_v), page_table, prm, 8)
    new_rows = lambda a: jnp.pad(a.reshape(Bs, Ts, KV_W), ((0, 0), (0, 8 * BLOCK - Ts), (0, 0)))
    kc_new, vc_new = _compress(new_rows(f["k_cmp"]), new_rows(f["v_cmp"]), prm, 8 * BLOCK)
    nb = kc_past.shape[1] + kc_new.shape[1]
    nbp = -(-nb // 128) * 128
    cat = lambda a, b: jnp.pad(jnp.concatenate([a, b], axis=1), ((0, 0), (0, nbp - nb), (0, 0)))
    kcb = cat(kc_past, kc_new).astype(BF16)
    vcTb = jnp.swapaxes(cat(vc_past, vc_new), 1, 2).astype(BF16)

    def lanes(a, rows):
        a = a.reshape(rows + (Bs, Ts))
        a = jnp.moveaxis(a, len(rows), 0)
        return jnp.pad(a, ((0, 0),) * (len(rows) + 1) + ((0, tqs - Ts),))

    def q_operand(qT):
        q = lanes(qT[0], (N_KV, GROUP, HEAD_DIM))
        q = jnp.moveaxis(q, 2, 3).reshape(Bs, N_KV, HEAD_DIM, GROUP * tqs)
        z = jnp.zeros_like(q[:, 0])
        return jnp.stack([jnp.concatenate([q[:, 0], z], axis=1), jnp.concatenate([z, q[:, 1]], axis=1)], axis=1)

    g = lanes(f["ngT"][0, :3 * N_HEADS], (N_KV, GROUP, 3))
    g = jnp.moveaxis(g, 2, 3).reshape(Bs, N_KV, 3, GROUP * tqs)
    gates = jnp.pad(g, ((0, 0), (0, 0), (0, 5), (0, 0)))
    rows_nat = lambda a: jnp.pad(a.reshape(Bs, Ts, KV_W), ((0, 0), (0, tqs - Ts), (0, 0)))
    rows_T = lambda a: lanes(a[0], (KV_W,))
    oT = _nsa_sample(q_operand(f["qp"]), q_operand(f["qr"]), kcb, vcTb, gates,
                     pool(cache_sel_k), pool(cache_sel_v), page_table, rows_nat(f["kselb"]), rows_T(f["vselT"]),
                     cache_win_k.reshape(Bs, -1, KV_W), cache_win_v.reshape(Bs, -1, KV_W),
                     rows_nat(f["kwinb"]), rows_T(f["vwinT"]), tqs, 4)
    o = oT.reshape(Bs, N_KV, HEAD_DIM, GROUP, tqs)[..., :Ts]
    o = jnp.transpose(o, (0, 4, 1, 3, 2)).reshape(1, n_tok, Q_W).astype(BF16)
    y = _back(x.reshape(1, n_tok, D_MODEL), o, u_out.reshape(1, n_tok, D_MODEL), f["gmg"], mod_tok, prm, n_tok)
    hs = lambda a: a.reshape(Bs, Ts, N_KV, HEAD_DIM)
    buf = cache_win_k.shape[1]
    keep = min(WINDOW, buf + Ts)
    kw = jnp.concatenate([cache_win_k, hs(f["k_win"])], axis=1)[:, -keep:]
    vw = jnp.concatenate([cache_win_v, hs(f["v_win"])], axis=1)[:, -keep:]
    return y.reshape(Bs, Ts, D_MODEL), (hs(f["k_cmp"]), hs(f["v_cmp"]), hs(f["k_sel"]), hs(f["v_sel"]), kw, vw, conv_state)


def kernel(x_prompt, x_sample, cache_cmp_k, cache_cmp_v, cache_sel_k, cache_sel_v, cache_win_k, cache_win_v,
           state_conv, page_table, c_prompt, c_sample, w_ada, b_ada, mix_norm_g, w_in, q_norm_g, k_norm_g,
           w_dw, b_dw, conv_ln_g, conv_ln_b, w_pw2, cmp_mod_k, cmp_w_k, cmp_mod_v, cmp_w_v, w_nsa_o, w_out,
           ffn_norm_g, w_gate, w_up, w_down):
    depth = w_in.shape[0]
    yp, ys = x_prompt, x_sample
    st_p, st_s = [], []
    for l in range(depth):
        p = dict(w_ada=w_ada[l], b_ada=b_ada[l], mix_norm_g=mix_norm_g[l], w_in=w_in[l],
                 q_norm_g=q_norm_g[l], k_norm_g=k_norm_g[l], w_dw=w_dw[l], b_dw=b_dw[l],
                 conv_ln_g=conv_ln_g[l], conv_ln_b=conv_ln_b[l], w_pw2=w_pw2[l],
                 cmp_mod_k=cmp_mod_k[l], cmp_w_k=cmp_w_k[l], cmp_mod_v=cmp_mod_v[l],
                 cmp_w_v=cmp_w_v[l], w_nsa_o=w_nsa_o[l], w_out=w_out[l],
                 ffn_norm_g=ffn_norm_g[l], w_gate=w_gate[l], w_up=w_up[l], w_down=w_down[l])
        prm = _prep_params(p)
        nb_p = c_prompt.shape[0]
        mod = _ada(jnp.concatenate([c_prompt, c_sample], axis=0), p["w_ada"], p["b_ada"])
        yp, sp = _prompt_layer(yp, mod[:nb_p], prm, 256, 128)
        ys, ss = _sample_layer(ys, mod[nb_p:], cache_cmp_k[l], cache_cmp_v[l], cache_sel_k[l], cache_sel_v[l],
                               cache_win_k[l], cache_win_v[l], state_conv[l], page_table, prm)
        st_p.append(sp)
        st_s.append(ss)
    stk = lambda lst, j: jnp.stack([s[j] for s in lst])
    return (yp, ys) + tuple(stk(st_p, j) for j in range(7)) + tuple(stk(st_s, j) for j in range(7))
```

```python
import functools
import math

import jax
import jax.numpy as jnp
from jax import lax
from jax.experimental import pallas as pl
from jax.experimental.pallas import tpu as pltpu

F32 = jnp.float32
BF16 = jnp.bfloat16

D_MODEL = 1024
D_CONV = 512
CONV_K = 31
N_HEADS = 8
N_KV = 2
HEAD_DIM = 64
GROUP = 4
Q_W = N_HEADS * HEAD_DIM
KV_W = N_KV * HEAD_DIM
BLOCK = 64
N_SEL = 16
WINDOW = 512
ROPE_THETA = 10000.0
D_FF = 2816
EPS = 1e-6
FORCED_BONUS = 2.0 * GROUP
PAGE = 128
OFF_Q = 2 * D_CONV
OFF_KV = OFF_Q + Q_W
OFF_NG = OFF_KV + 6 * KV_W
OFF_MG = OFF_NG + 3 * N_HEADS
Q_SCALE = HEAD_DIM ** -0.5 * math.log2(math.e)
NEG = -1e30
N_T_ROWS = 1312
CONV_HALO = 32
SUBLANES = 8
LANES = 128
SEL_CHUNK = 512
VMEM_LIMIT = 56 * 1024 * 1024


def _cparams(sem):
    return pltpu.CompilerParams(dimension_semantics=sem, vmem_limit_bytes=VMEM_LIMIT)


def _ada_kernel(c_ref, w_ref, b_ref, o_ref):
    c = c_ref[...]
    s = c * jax.nn.sigmoid(c)
    o_ref[...] = jnp.dot(s.astype(BF16), w_ref[...].astype(BF16), preferred_element_type=F32) + b_ref[...]


def _ada(c, w_ada, b_ada):
    n, tn = c.shape[0], 1024
    return pl.pallas_call(
        _ada_kernel,
        grid=(w_ada.shape[1] // tn,),
        in_specs=[pl.BlockSpec((n, D_MODEL), lambda j: (0, 0)),
                  pl.BlockSpec((D_MODEL, tn), lambda j: (0, j)),
                  pl.BlockSpec((1, tn), lambda j: (0, j))],
        out_specs=pl.BlockSpec((n, tn), lambda j: (0, j)),
        out_shape=jax.ShapeDtypeStruct((n, w_ada.shape[1]), F32),
        compiler_params=_cparams(("arbitrary",)),
        name="ada",
    )(c, w_ada, b_ada.reshape(1, -1))


def _front_kernel(x_ref, sh1_ref, sc1_ref, g_ref, wn_ref, wt_ref, gq_ref, gk_ref, cos_ref, sin_ref,
                  u_ref, gmg_ref, qp_ref, qr_ref, kcmpT_ref, vcmpT_ref, kselT_ref, vselT_ref, kwinT_ref, vwinT_ref,
                  kselb_ref, kwinb_ref, vselTb_ref, vwinTb_ref, ngT_ref):
    x = x_ref[0]
    ms = jnp.mean(x * x, axis=-1, keepdims=True)
    h = x * lax.rsqrt(ms + EPS) * g_ref[...]
    h = h * (1.0 + sc1_ref[0]) + sh1_ref[0]
    hb = h.astype(BF16)
    zn = jnp.dot(hb, wn_ref[...], preferred_element_type=F32)
    u_ref[0] = zn[:, :D_CONV] * jax.nn.sigmoid(zn[:, D_CONV:2 * D_CONV])
    gmg_ref[0] = jax.nn.sigmoid(zn[:, 2 * D_CONV:])
    zt = lax.dot_general(wt_ref[...], hb, (((1,), (1,)), ((), ())), preferred_element_type=F32)
    cos = cos_ref[...]
    sin = sin_ref[...]
    half = HEAD_DIM // 2

    def norm(v, g):
        m = jnp.mean(v * v, axis=0, keepdims=True)
        return v * lax.rsqrt(m + EPS) * g

    def rope(v):
        a, b = v[:half], v[half:]
        return jnp.concatenate([a * cos - b * sin, b * cos + a * sin], axis=0)

    gq = gq_ref[...]
    for hh in range(N_HEADS):
        q = norm(zt[hh * HEAD_DIM:(hh + 1) * HEAD_DIM], gq) * Q_SCALE
        qp_ref[0, hh * HEAD_DIM:(hh + 1) * HEAD_DIM, :] = q.astype(BF16)
        qr_ref[0, hh * HEAD_DIM:(hh + 1) * HEAD_DIM, :] = rope(q).astype(BF16)

    def head_pair(j, g=None, rotary=False):
        outs = []
        for hh in range(N_KV):
            lo = Q_W + j * KV_W + hh * HEAD_DIM
            v = zt[lo:lo + HEAD_DIM]
            if g is not None:
                v = norm(v, g)
            if rotary:
                v = rope(v)
            outs.append(v)
        return jnp.concatenate(outs, axis=0)

    k_sel = head_pair(2, gk_ref[1], True)
    v_sel = head_pair(3)
    k_win = head_pair(4, gk_ref[2], True)
    v_win = head_pair(5)
    kcmpT_ref[0] = head_pair(0, gk_ref[0])
    vcmpT_ref[0] = head_pair(1)
    kselT_ref[0] = k_sel
    vselT_ref[0] = v_sel
    kwinT_ref[0] = k_win
    vwinT_ref[0] = v_win
    kselb_ref[0] = k_sel.T.astype(BF16)
    kwinb_ref[0] = k_win.T.astype(BF16)
    vselTb_ref[0] = v_sel.astype(BF16)
    vwinTb_ref[0] = v_win.astype(BF16)
    ngT_ref[0] = jax.nn.sigmoid(zt[Q_W + 6 * KV_W:])


def _front(x, mod, pos, prm, tm):
    B, T, _ = x.shape
    Tm = mod.shape[1]
    tmm = 1 if Tm == 1 else tm
    half = HEAD_DIM // 2
    inv = jnp.power(ROPE_THETA, -jnp.arange(half, dtype=F32) / half)
    ang = pos.astype(F32)[:, None] * inv[None, :]
    cosT, sinT = jnp.cos(ang).T, jnp.sin(ang).T
    gq = jnp.broadcast_to(prm["q_norm_g"][:, None], (HEAD_DIM, tm))
    gk = jnp.broadcast_to(prm["k_norm_g"][:, :, None], (3, HEAD_DIM, tm))
    nt = T // tm
    tok = lambda w: pl.BlockSpec((1, tm, w), lambda b, t: (b, t, 0))
    tokT = lambda r: pl.BlockSpec((1, r, tm), lambda b, t: (b, 0, t))
    modspec = lambda k: pl.BlockSpec((1, tmm, D_MODEL), (lambda b, t: (b, t, k)) if Tm != 1 else (lambda b, t: (b, 0, k)))
    const = lambda shape: pl.BlockSpec(shape, lambda b, t: (0,) * len(shape))
    sd = jax.ShapeDtypeStruct
    outs = pl.pallas_call(
        _front_kernel,
        grid=(B, nt),
        in_specs=[tok(D_MODEL), modspec(0), modspec(1), const((1, D_MODEL)),
                  const((D_MODEL, 3 * D_MODEL)), const((N_T_ROWS, D_MODEL)),
                  const((HEAD_DIM, tm)), const((3, HEAD_DIM, tm)),
                  pl.BlockSpec((half, tm), lambda b, t: (0, t)), pl.BlockSpec((half, tm), lambda b, t: (0, t))],
        out_specs=[tok(D_CONV), tok(2 * D_MODEL), tokT(Q_W), tokT(Q_W)] + [tokT(KV_W)] * 6
                  + [tok(KV_W), tok(KV_W), tokT(KV_W), tokT(KV_W), tokT(32)],
        out_shape=[sd((B, T, D_CONV), F32), sd((B, T, 2 * D_MODEL), F32), sd((B, Q_W, T), BF16), sd((B, Q_W, T), BF16)]
                  + [sd((B, KV_W, T), F32)] * 6
                  + [sd((B, T, KV_W), BF16), sd((B, T, KV_W), BF16), sd((B, KV_W, T), BF16), sd((B, KV_W, T), BF16),
                     sd((B, 32, T), F32)],
        compiler_params=_cparams(("parallel", "arbitrary")),
        name="front",
    )(x, mod, mod, prm["mix_norm_g"].reshape(1, -1), prm["wn"], prm["wt"], gq, gk, cosT, sinT)
    names = ("u", "gmg", "qp", "qr", "kcmpT", "vcmpT", "kselT", "vselT", "kwinT", "vwinT",
             "kselb", "kwinb", "vselTb", "vwinTb", "ngT")
    return dict(zip(names, outs))


def _conv_kernel(u_ref, st_ref, wdw_ref, bdw_ref, lng_ref, lnb_ref, wpw_ref, uo_ref, so_ref, ext_ref, sh_ref, *, tm):
    t = pl.program_id(1)
    pad = CONV_HALO - (CONV_K - 1)
    n_sh = tm + CONV_HALO - SUBLANES

    @pl.when(t == 0)
    def _():
        ext_ref[0:pad, :] = jnp.zeros((pad, D_CONV), F32)
        ext_ref[pad:CONV_HALO, :] = st_ref[0]

    ext_ref[CONV_HALO:CONV_HALO + tm, :] = u_ref[0]
    for r in range(1, SUBLANES):
        sh_ref[r - 1] = ext_ref[r:r + n_sh, :]
    y = jnp.zeros((tm, D_CONV), F32) + bdw_ref[...]
    for k in range(CONV_K):
        a, r = divmod(pad + k, SUBLANES)
        rows = ext_ref[SUBLANES * a:SUBLANES * a + tm, :] if r == 0 else sh_ref[r - 1, SUBLANES * a:SUBLANES * a + tm, :]
        y = y + rows * wdw_ref[k:k + 1, :]
    mu = jnp.mean(y, axis=-1, keepdims=True)
    yc = y - mu
    var = jnp.mean(yc * yc, axis=-1, keepdims=True)
    yn = yc * lax.rsqrt(var + EPS) * lng_ref[...] + lnb_ref[...]
    ya = yn * jax.nn.sigmoid(yn)
    uo_ref[0] = jnp.dot(ya.astype(BF16), wpw_ref[...], preferred_element_type=F32)
    so_ref[0] = ext_ref[tm + pad:tm + CONV_HALO, :]
    carry = ext_ref[tm:tm + CONV_HALO, :]
    ext_ref[0:CONV_HALO, :] = carry


def _conv(u, state, prm, tm):
    B, T, _ = u.shape
    const = lambda shape: pl.BlockSpec(shape, lambda b, t: (0,) * len(shape))
    return pl.pallas_call(
        functools.partial(_conv_kernel, tm=tm),
        grid=(B, T // tm),
        in_specs=[pl.BlockSpec((1, tm, D_CONV), lambda b, t: (b, t, 0)),
                  pl.BlockSpec((1, CONV_K - 1, D_CONV), lambda b, t: (b, 0, 0)),
                  const((CONV_K, D_CONV)), const((1, D_CONV)), const((1, D_CONV)), const((1, D_CONV)),
                  const((D_CONV, D_MODEL))],
        out_specs=[pl.BlockSpec((1, tm, D_MODEL), lambda b, t: (b, t, 0)),
                   pl.BlockSpec((1, CONV_K - 1, D_CONV), lambda b, t: (b, 0, 0))],
        out_shape=[jax.ShapeDtypeStruct((B, T, D_MODEL), F32), jax.ShapeDtypeStruct((B, CONV_K - 1, D_CONV), F32)],
        scratch_shapes=[pltpu.VMEM((CONV_HALO + tm, D_CONV), F32),
                        pltpu.VMEM((SUBLANES - 1, tm + CONV_HALO - SUBLANES, D_CONV), F32)],
        compiler_params=_cparams(("parallel", "arbitrary")),
        name="conv",
    )(u, state, prm["w_dw"], prm["b_dw"].reshape(1, -1), prm["conv_ln_g"].reshape(1, -1),
      prm["conv_ln_b"].reshape(1, -1), prm["w_pw2b"])


def _compress_kernel(*refs, n_in, n_prefetch, group):
    refs = refs[n_prefetch:]
    k_refs, v_refs = refs[:n_in], refs[n_in:2 * n_in]
    modk_ref, wk_ref, modv_ref, wv_ref, s_ref, kc_ref, vc_ref, acc_ref = refs[2 * n_in:]
    j = pl.program_id(1)
    slot = j % group

    @pl.when(slot == 0)
    def _():
        acc_ref[...] = jnp.zeros_like(acc_ref)

    place = s_ref[slot]

    def one(i, x_refs, mod_ref, w_ref, out_ref):
        xT = jnp.concatenate([r[0] for r in x_refs], axis=1) if n_in > 1 else x_refs[0][0]
        y = xT * (1.0 + mod_ref[...])
        hi = y.astype(BF16)
        lo = (y - hi.astype(F32)).astype(BF16)
        acc_ref[i] += (jnp.dot(hi, place, preferred_element_type=F32) + jnp.dot(lo, place, preferred_element_type=F32))
        out_ref[0] = jnp.dot(w_ref[...], acc_ref[i].astype(BF16), preferred_element_type=F32)

    one(0, k_refs, modk_ref, wk_ref, kc_ref)
    one(1, v_refs, modv_ref, wv_ref, vc_ref)


def _compress_consts(prm, cols):
    nbs = max(cols // BLOCK, 1)
    group = LANES // nbs
    modT = lambda m: jnp.tile(m.T, (N_KV, max(cols // BLOCK, 1)))[:, :cols]
    wT = lambda w: jnp.kron(jnp.eye(N_KV, dtype=F32), w.T).astype(BF16)
    r = jnp.arange(cols)[None, :, None]
    g = jnp.arange(group)[:, None, None]
    lane = jnp.arange(LANES)[None, None, :]
    place = jnp.where(lane == g * nbs + r // BLOCK, 1.0 / BLOCK, 0.0).astype(BF16)
    return (modT(prm["cmp_mod_k"]), wT(prm["cmp_w_k"]), modT(prm["cmp_mod_v"]), wT(prm["cmp_w_v"]), place), group


def _compress_call(kT, vT, page_table, prm, cols, n_in, x_spec, grid, n_out_blocks):
    consts, group = _compress_consts(prm, cols * n_in)
    n_prefetch = 0 if page_table is None else 1
    const = lambda shape: pl.BlockSpec(shape, lambda *a: (0,) * len(shape))
    ospec = pl.BlockSpec((1, KV_W, LANES), lambda b, j, *a: (b, 0, j // group))
    osh = jax.ShapeDtypeStruct((grid[0], KV_W, n_out_blocks * LANES), F32)
    in_specs = ([x_spec(i) for i in range(n_in)] * 2
                + [const((KV_W, cols * n_in)), const((KV_W, KV_W)), const((KV_W, cols * n_in)), const((KV_W, KV_W)),
                   const((group, cols * n_in, LANES))])
    scratch = [pltpu.VMEM((2, KV_W, LANES), F32)]
    kern = functools.partial(_compress_kernel, n_in=n_in, n_prefetch=n_prefetch, group=group)
    if page_table is None:
        call = pl.pallas_call(kern, grid=grid, in_specs=in_specs, out_specs=[ospec, ospec], out_shape=[osh, osh],
                              scratch_shapes=scratch, compiler_params=_cparams(("parallel", "arbitrary")), name="compress")
        return call(*([kT] * n_in), *([vT] * n_in), *consts)
    grid_spec = pltpu.PrefetchScalarGridSpec(num_scalar_prefetch=1, grid=grid, in_specs=in_specs,
                                             out_specs=[ospec, ospec], scratch_shapes=scratch)
    call = pl.pallas_call(kern, grid_spec=grid_spec, out_shape=[osh, osh],
                          compiler_params=_cparams(("parallel", "arbitrary")), name="compress_paged")
    return call(page_table.reshape(-1), *([kT] * n_in), *([vT] * n_in), *consts)


def _compress(kT, vT, prm, cols):
    B, _, L = kT.shape
    cols = min(cols, L)
    n_blocks = -(-(L // BLOCK) // LANES) if L >= BLOCK else 1
    x_spec = lambda i: pl.BlockSpec((1, KV_W, cols), lambda b, j: (b, 0, j))
    return _compress_call(kT, vT, None, prm, cols, 1, x_spec, (B, L // cols), n_blocks)


def _compress_paged(cache_kT, cache_vT, page_table, prm, pages_per_step):
    B, n_pages = page_table.shape
    x_spec = lambda i: pl.BlockSpec((1, KV_W, PAGE), lambda b, j, pt: (pt[b * n_pages + j * pages_per_step + i], 0, 0))
    n_blocks = -(-(n_pages * PAGE // BLOCK) // LANES)
    return _compress_call(cache_kT, cache_vT, page_table, prm, PAGE, pages_per_step, x_spec,
                          (B, n_pages // pages_per_step), n_blocks)


def _attend_scores(s, vT, mask, carry):
    m, l, acc = carry
    s = jnp.where(mask, s, NEG)
    m_new = jnp.maximum(m, jnp.max(s, axis=0, keepdims=True))
    alpha = jnp.exp2(m - m_new)
    p = jnp.exp2(s - m_new)
    l = alpha * l + jnp.sum(p, axis=0, keepdims=True)
    acc = alpha * acc + jnp.dot(vT, p.astype(BF16), preferred_element_type=F32)
    return m_new, l, acc


def _attend(qT, k, vT, mask, carry):
    return _attend_scores(jnp.dot(k, qT, preferred_element_type=F32), vT, mask, carry)


def _attend_init(L):
    return jnp.full((1, L), NEG, F32), jnp.zeros((1, L), F32), jnp.zeros((HEAD_DIM, L), F32)


def _attend_out(carry):
    _, l, acc = carry
    return acc / jnp.maximum(l, 1e-30)


def _block_mask(sel_rows, kc):
    nb, L = sel_rows.shape
    return jnp.broadcast_to((sel_rows > 0.5)[:, None, :], (nb, BLOCK, L)).reshape(kc, L)


def _cmp_and_select(qpT, kc, vcT, qpos, tq):
    nb, L = kc.shape[0], qpT.shape[1]
    s = jnp.dot(kc, qpT, preferred_element_type=F32)
    blk = lax.broadcasted_iota(jnp.int32, (nb, L), 0)
    ok = (blk + 1) * BLOCK <= qpos + 1
    s = jnp.where(ok, s, -jnp.inf)
    m = jnp.max(s, axis=0, keepdims=True)
    m = jnp.where(m > -jnp.inf, m, 0.0)
    e = jnp.where(ok, jnp.exp2(s - m), 0.0)
    p = e / jnp.maximum(jnp.sum(e, axis=0, keepdims=True), 1e-30)
    o_cmp = jnp.dot(vcT, p.astype(BF16), preferred_element_type=F32)
    if tq % LANES == 0:
        w = tq
        imp = p[:, :tq]
        for g in range(1, GROUP):
            imp = imp + p[:, g * tq:(g + 1) * tq]
    else:
        w = L
        imp = p
        for g in range(1, GROUP):
            imp = imp + pltpu.roll(p, g * tq, 1)
    blk, qpos = blk[:, :w], qpos[:, :w]
    cur = lax.shift_right_logical(qpos, 6)
    visible = blk <= cur
    forced = (blk == 0) | (blk == cur) | (blk == cur - 1)
    score = jnp.where(visible, imp + jnp.where(forced, FORCED_BONUS, 0.0), -1.0)
    ridx = blk.astype(F32)
    sel = jnp.zeros_like(score)
    rem = score
    for _ in range(min(N_SEL, nb)):
        mx = jnp.max(rem, axis=0, keepdims=True)
        first = jnp.min(jnp.where(rem == mx, ridx, float(nb)), axis=0, keepdims=True)
        pick = ridx == first
        rem = jnp.where(pick, -jnp.inf, rem)
        sel = jnp.where(pick, 1.0, sel)
    if w != L:
        sel = jnp.concatenate([sel] * GROUP, axis=1)
    return o_cmp, sel


def _nsa_prompt_kernel(qp_ref, qr_ref, kc_ref, vcT_ref, ksel_ref, vselT_ref, kwin_ref, vwinT_ref, ngT_ref,
                       o_ref, sel_ref, *, tq):
    t = pl.program_id(1)
    s0 = t * tq
    L = GROUP * tq
    lane = lax.broadcasted_iota(jnp.int32, (1, L), 1)
    qpos = s0 + lane % tq
    ng = ngT_ref[0]
    heads = [slice(h * HEAD_DIM, (h + 1) * HEAD_DIM) for h in range(N_KV)]

    def qpad(ref, h):
        q = jnp.concatenate([ref[0, (h * GROUP + g) * HEAD_DIM:(h * GROUP + g + 1) * HEAD_DIM, :]
                             for g in range(GROUP)], axis=1)
        z = jnp.zeros_like(q)
        return jnp.concatenate([q, z] if h == 0 else [z, q], axis=0)

    qr = [qpad(qr_ref, h) for h in range(N_KV)]
    o_cmp = []
    for h in range(N_KV):
        o, sel = _cmp_and_select(qpad(qp_ref, h), kc_ref[0], vcT_ref[0, heads[h], :], qpos, tq)
        sel_ref[h] = sel
        o_cmp.append(o)

    def sel_steps(chunks, carries, causal_last):
        k0s = [pl.multiple_of(c * SEL_CHUNK, SEL_CHUNK) for c in chunks]
        scores = [[jnp.dot(ksel_ref[0, pl.ds(k0, SEL_CHUNK), :], qr[h], preferred_element_type=F32)
                   for h in range(N_KV)] for k0 in k0s]
        carries = list(carries)
        for i, (c, k0) in enumerate(zip(chunks, k0s)):
            causal = causal_last and i == len(chunks) - 1
            r0 = pl.multiple_of(c * (SEL_CHUNK // BLOCK), SEL_CHUNK // BLOCK)
            if causal:
                visible = k0 + lax.broadcasted_iota(jnp.int32, (SEL_CHUNK, L), 0) <= qpos
            for h in range(N_KV):
                mask = _block_mask(sel_ref[h, pl.ds(r0, SEL_CHUNK // BLOCK), :], SEL_CHUNK)
                if causal:
                    mask = mask & visible
                carries[h] = _attend_scores(scores[i][h], vselT_ref[0, heads[h], pl.ds(k0, SEL_CHUNK)], mask, carries[h])
        return tuple(carries)

    n_full = s0 // SEL_CHUNK
    carries = lax.fori_loop(0, n_full // 2, lambda i, cr: sel_steps([2 * i, 2 * i + 1], cr, False),
                            (_attend_init(L),) * N_KV)
    carries = lax.cond(n_full % 2 == 1, lambda cr: sel_steps([n_full - 1], cr, False), lambda cr: cr, carries)
    carries = sel_steps([n_full], carries, True)

    w0 = pl.multiple_of(jnp.maximum(s0 - WINDOW, 0), tq)
    kpos = w0 + lax.broadcasted_iota(jnp.int32, (WINDOW + tq, L), 0)
    wmask = (kpos <= qpos) & (kpos > qpos - WINDOW)
    kw = kwin_ref[0, pl.ds(w0, WINDOW + tq), :]

    def gate(h, br):
        return jnp.concatenate([ng[h * 12 + g * 3 + br:h * 12 + g * 3 + br + 1, :] for g in range(GROUP)], axis=1)

    for h in range(N_KV):
        o_sel = _attend_out(carries[h])
        o_win = _attend_out(_attend(qr[h], kw, vwinT_ref[0, heads[h], pl.ds(w0, WINDOW + tq)], wmask, _attend_init(L)))
        o = gate(h, 0) * o_cmp[h] + gate(h, 1) * o_sel + gate(h, 2) * o_win
        oT = jnp.concatenate([o[:, g * tq:(g + 1) * tq] for g in range(GROUP)], axis=0)
        o_ref[0, :, h * GROUP * HEAD_DIM:(h + 1) * GROUP * HEAD_DIM] = oT.T.astype(o_ref.dtype)


def _nsa_prompt(f, kcb, vcTb, tq):
    B, _, T = f["qp"].shape
    assert T % SEL_CHUNK == 0 and SEL_CHUNK % tq == 0 and WINDOW % tq == 0 and T >= WINDOW + tq
    nbp = kcb.shape[1]
    full = lambda shape: pl.BlockSpec((1,) + shape, lambda b, t: (b, 0, 0))
    return pl.pallas_call(
        functools.partial(_nsa_prompt_kernel, tq=tq),
        grid=(B, T // tq),
        in_specs=[pl.BlockSpec((1, Q_W, tq), lambda b, t: (b, 0, t)), pl.BlockSpec((1, Q_W, tq), lambda b, t: (b, 0, t)),
                  full((nbp, KV_W)), full((KV_W, nbp)),
                  full((T, KV_W)), full((KV_W, T)), full((T, KV_W)), full((KV_W, T)),
                  pl.BlockSpec((1, 32, tq), lambda b, t: (b, 0, t))],
        out_specs=pl.BlockSpec((1, tq, Q_W), lambda b, t: (b, t, 0)),
        out_shape=jax.ShapeDtypeStruct((B, T, Q_W), BF16),
        scratch_shapes=[pltpu.VMEM((N_KV, nbp, GROUP * tq), F32)],
        compiler_params=_cparams(("parallel", "arbitrary")),
        name="nsa_prompt",
    )(f["qp"], f["qr"], kcb, vcTb, f["kselb"], f["vselTb"], f["kwinb"], f["vwinTb"], f["ngT"])


def _nsa_sample_kernel(pt_ref, qp_ref, qr_ref, kc_ref, vcT_ref, gate_ref, *refs, tqs, past_len, pages_per_step):
    del pt_ref
    kpages, vpages = refs[:pages_per_step], refs[pages_per_step:2 * pages_per_step]
    (knew_ref, vnewT_ref, cwkT_ref, cwvT_ref, kwnew_ref, vwnewT_ref,
     o_ref, sel_ref, ocmp_ref, m_ref, l_ref, acc_ref) = refs[2 * pages_per_step:]
    j = pl.program_id(1)
    n_steps = pl.num_programs(1)
    L = GROUP * tqs
    kc_step = pages_per_step * PAGE
    lane = lax.broadcasted_iota(jnp.int32, (1, L), 1)
    qi = lane % tqs
    qpos = past_len + qi
    heads = [slice(h * HEAD_DIM, (h + 1) * HEAD_DIM) for h in range(N_KV)]

    @pl.when(j == 0)
    def _():
        for h in range(N_KV):
            o_cmp, sel = _cmp_and_select(qp_ref[0, h], kc_ref[0], vcT_ref[0, heads[h], :], qpos, tqs)
            sel_ref[h] = sel
            ocmp_ref[h] = o_cmp
            m, l, acc = _attend_init(L)
            m_ref[h], l_ref[h], acc_ref[h] = m, l, acc

    k = jnp.concatenate([r[0] for r in kpages], axis=1).T.astype(BF16)
    vT = jnp.concatenate([r[0] for r in vpages], axis=1).astype(BF16)
    r0 = pl.multiple_of(j * (kc_step // BLOCK), kc_step // BLOCK)
    for h in range(N_KV):
        mask = _block_mask(sel_ref[h, pl.ds(r0, kc_step // BLOCK), :], kc_step)
        m, l, acc = _attend(qr_ref[0, h], k, vT[heads[h]], mask, (m_ref[h], l_ref[h], acc_ref[h]))
        m_ref[h], l_ref[h], acc_ref[h] = m, l, acc

    @pl.when(j == n_steps - 1)
    def _():
        nb_past = past_len // BLOCK
        buf = cwkT_ref.shape[2]
        r_new = lax.broadcasted_iota(jnp.int32, (tqs, L), 0)
        r_buf = lax.broadcasted_iota(jnp.int32, (buf, L), 0)
        cwk = cwkT_ref[0].T.astype(BF16)
        cwvT = cwvT_ref[0].astype(BF16)
        gates = gate_ref[0]
        for h in range(N_KV):
            qr = qr_ref[0, h]
            mask = (sel_ref[h, nb_past:nb_past + 1, :] > 0.5) & (r_new <= qi)
            carry = _attend(qr, knew_ref[0], vnewT_ref[0, heads[h], :], mask, (m_ref[h], l_ref[h], acc_ref[h]))
            o_sel = _attend_out(carry)
            carry = _attend(qr, cwk, cwvT[heads[h]], r_buf > qi + (buf - WINDOW), _attend_init(L))
            carry = _attend(qr, kwnew_ref[0], vwnewT_ref[0, heads[h], :], r_new <= qi, carry)
            o_win = _attend_out(carry)
            g = gates[h]
            o_ref[0, h] = g[0:1] * ocmp_ref[h] + g[1:2] * o_sel + g[2:3] * o_win


def _nsa_sample(qpT, qrT, kcb, vcTb, gates, cache_sel_kT, cache_sel_vT, page_table, knew, vnewT,
                cache_win_kT, cache_win_vT, kwnew, vwnewT, tqs, pages_per_step):
    Bs, n_pages = page_table.shape
    L = GROUP * tqs
    nbp = kcb.shape[1]
    buf = cache_win_kT.shape[2]
    per_b = lambda shape: pl.BlockSpec((1,) + shape, lambda b, j, pt: (b,) + (0,) * len(shape))
    pspec = lambda i: pl.BlockSpec((1, KV_W, PAGE), lambda b, j, pt: (pt[b * n_pages + j * pages_per_step + i], 0, 0))
    grid_spec = pltpu.PrefetchScalarGridSpec(
        num_scalar_prefetch=1,
        grid=(Bs, n_pages // pages_per_step),
        in_specs=[per_b((N_KV, KV_W, L)), per_b((N_KV, KV_W, L)), per_b((nbp, KV_W)), per_b((KV_W, nbp)),
                  per_b((N_KV, 8, L))]
                 + [pspec(i) for i in range(pages_per_step)] * 2
                 + [per_b((tqs, KV_W)), per_b((KV_W, tqs)), per_b((KV_W, buf)), per_b((KV_W, buf)),
                    per_b((tqs, KV_W)), per_b((KV_W, tqs))],
        out_specs=per_b((N_KV, HEAD_DIM, L)),
        scratch_shapes=[pltpu.VMEM((N_KV, nbp, L), F32), pltpu.VMEM((N_KV, HEAD_DIM, L), F32),
                        pltpu.VMEM((N_KV, 1, L), F32), pltpu.VMEM((N_KV, 1, L), F32),
                        pltpu.VMEM((N_KV, HEAD_DIM, L), F32)],
    )
    return pl.pallas_call(
        functools.partial(_nsa_sample_kernel, tqs=tqs, past_len=n_pages * PAGE, pages_per_step=pages_per_step),
        grid_spec=grid_spec,
        out_shape=jax.ShapeDtypeStruct((Bs, N_KV, HEAD_DIM, L), F32),
        compiler_params=_cparams(("parallel", "arbitrary")),
        name="nsa_sample",
    )(page_table.reshape(-1), qpT, qrT, kcb, vcTb, gates,
      *([cache_sel_kT] * pages_per_step), *([cache_sel_vT] * pages_per_step),
      knew, vnewT, cache_win_kT, cache_win_vT, kwnew, vwnewT)


def _back_kernel(x_ref, o_ref, uo_ref, gmg_ref, g1_ref, sh2_ref, sc2_ref, g2_ref, fg_ref,
                 wno_ref, wout_ref, wg_ref, wu_ref, wd_ref, y_ref):
    nsa = jnp.dot(o_ref[0], wno_ref[...], preferred_element_type=F32)
    gmg = gmg_ref[0]
    merged = gmg[:, :D_MODEL] * uo_ref[0] + gmg[:, D_MODEL:] * nsa
    mix = jnp.dot(merged.astype(BF16), wout_ref[...], preferred_element_type=F32)
    x1 = x_ref[0] + g1_ref[0] * mix
    ms = jnp.mean(x1 * x1, axis=-1, keepdims=True)
    h = x1 * lax.rsqrt(ms + EPS) * fg_ref[...]
    h = h * (1.0 + sc2_ref[0]) + sh2_ref[0]
    hb = h.astype(BF16)
    a = jnp.dot(hb, wg_ref[...], preferred_element_type=F32)
    b = jnp.dot(hb, wu_ref[...], preferred_element_type=F32)
    act = a * jax.nn.sigmoid(a) * b
    ffn = jnp.dot(act.astype(BF16), wd_ref[...], preferred_element_type=F32)
    y_ref[0] = x1 + g2_ref[0] * ffn


def _back(x, o, u_out, gmg, mod, prm, tm):
    B, T, _ = x.shape
    Tm = mod.shape[1]
    tmm = 1 if Tm == 1 else tm
    tok = lambda w: pl.BlockSpec((1, tm, w), lambda b, t: (b, t, 0))
    modspec = lambda k: pl.BlockSpec((1, tmm, D_MODEL), (lambda b, t: (b, t, k)) if Tm != 1 else (lambda b, t: (b, 0, k)))
    wspec = lambda shape: pl.BlockSpec(shape, lambda b, t: (0, 0), pipeline_mode=pl.Buffered(1))
    return pl.pallas_call(
        _back_kernel,
        grid=(B, T // tm),
        in_specs=[tok(D_MODEL), tok(Q_W), tok(D_MODEL), tok(2 * D_MODEL),
                  modspec(2), modspec(3), modspec(4), modspec(5), wspec((1, D_MODEL)),
                  wspec((Q_W, D_MODEL)), wspec((D_MODEL, D_MODEL)), wspec((D_MODEL, D_FF)), wspec((D_MODEL, D_FF)),
                  wspec((D_FF, D_MODEL))],
        out_specs=tok(D_MODEL),
        out_shape=jax.ShapeDtypeStruct((B, T, D_MODEL), F32),
        compiler_params=_cparams(("parallel", "arbitrary")),
        name="back",
    )(x, o, u_out, gmg, mod, mod, mod, mod, prm["ffn_norm_g"].reshape(1, -1),
      prm["w_nsa_ob"], prm["w_outb"], prm["w_gateb"], prm["w_upb"], prm["w_downb"])


def _prep_params(p):
    w_in = p["w_in"]
    q = dict(p)
    q["wn"] = jnp.concatenate([w_in[:, :OFF_Q], w_in[:, OFF_MG:]], axis=1).astype(BF16)
    wt = jnp.concatenate([w_in[:, OFF_Q:OFF_MG], jnp.zeros((D_MODEL, N_T_ROWS - (OFF_MG - OFF_Q)), F32)], axis=1)
    q["wt"] = wt.T.astype(BF16)
    q["w_pw2b"] = p["w_pw2"].astype(BF16)
    for name in ("w_nsa_o", "w_out", "w_gate", "w_up", "w_down"):
        q[name + "b"] = p[name].astype(BF16)
    return q


def _rows_from_channel_major(a):
    B, _, T = a.shape
    return jnp.transpose(a.reshape(B, N_KV, HEAD_DIM, T), (0, 3, 1, 2))


def _channel_major(a):
    N, R = a.shape[:2]
    return jnp.transpose(a, (0, 2, 3, 1)).reshape(N, KV_W, R)


def _prompt_layer(x, mod, prm, tm, tq):
    B, T, _ = x.shape
    mod = mod[:, None, :]
    f = _front(x, mod, jnp.arange(T, dtype=jnp.int32), prm, tm)
    u_out, conv_state = _conv(f["u"], jnp.zeros((B, CONV_K - 1, D_CONV), F32), prm, tm)
    kcT, vcT = _compress(f["kcmpT"], f["vcmpT"], prm, 2048)
    o = _nsa_prompt(f, jnp.swapaxes(kcT, 1, 2).astype(BF16), vcT.astype(BF16), tq)
    y = _back(x, o, u_out, f["gmg"], mod, prm, tm)
    keep = min(WINDOW, T)
    rows = _rows_from_channel_major
    return y, (rows(f["kcmpT"]), rows(f["vcmpT"]), rows(f["kselT"]), rows(f["vselT"]),
               rows(f["kwinT"][:, :, T - keep:]), rows(f["vwinT"][:, :, T - keep:]), conv_state)


def _sample_layer(x, mod, cache_cmp_k, cache_cmp_v, cache_sel_k, cache_sel_v, cache_win_k, cache_win_v,
                  state_conv, page_table, prm):
    Bs, Ts, _ = x.shape
    n_pages = page_table.shape[1]
    past_len = n_pages * PAGE
    n_tok = Bs * Ts
    tqs = 32
    assert n_tok % LANES == 0 and Ts <= tqs and n_pages % 8 == 0
    pos = past_len + jnp.arange(n_tok, dtype=jnp.int32) % Ts
    mod_tok = jnp.repeat(mod, Ts, axis=0)[None]
    f = _front(x.reshape(1, n_tok, D_MODEL), mod_tok, pos, prm, n_tok)
    u_out, conv_state = _conv(f["u"].reshape(Bs, Ts, D_CONV), state_conv, prm, Ts)

    def lanes(a, rows, width):
        a = a.reshape(rows + (Bs, Ts))
        a = jnp.moveaxis(a, len(rows), 0)
        return jnp.pad(a, ((0, 0),) * (len(rows) + 1) + ((0, width - Ts),))

    kcT_past, vcT_past = _compress_paged(_channel_major(cache_cmp_k), _channel_major(cache_cmp_v), page_table, prm, 8)
    kcT_new, vcT_new = _compress(lanes(f["kcmpT"][0], (KV_W,), BLOCK), lanes(f["vcmpT"][0], (KV_W,), BLOCK), prm, BLOCK)
    nb_past = past_len // BLOCK
    cat = lambda a, b: jnp.concatenate([a[:, :, :nb_past], b], axis=2)
    kcb = jnp.swapaxes(cat(kcT_past, kcT_new), 1, 2).astype(BF16)
    vcTb = cat(vcT_past, vcT_new).astype(BF16)

    def q_operand(qT):
        q = lanes(qT[0], (N_KV, GROUP, HEAD_DIM), tqs)
        q = jnp.moveaxis(q, 2, 3).reshape(Bs, N_KV, HEAD_DIM, GROUP * tqs)
        z = jnp.zeros_like(q[:, 0])
        return jnp.stack([jnp.concatenate([q[:, 0], z], axis=1), jnp.concatenate([z, q[:, 1]], axis=1)], axis=1)

    g = lanes(f["ngT"][0, :3 * N_HEADS], (N_KV, GROUP, 3), tqs)
    g = jnp.moveaxis(g, 2, 3).reshape(Bs, N_KV, 3, GROUP * tqs)
    gates = jnp.pad(g, ((0, 0), (0, 0), (0, 5), (0, 0)))
    rows_nat = lambda a: jnp.pad(a.reshape(Bs, Ts, KV_W), ((0, 0), (0, tqs - Ts), (0, 0)))
    rows_T = lambda a: lanes(a[0], (KV_W,), tqs)
    oT = _nsa_sample(q_operand(f["qp"]), q_operand(f["qr"]), kcb, vcTb, gates,
                     _channel_major(cache_sel_k), _channel_major(cache_sel_v), page_table,
                     rows_nat(f["kselb"]), rows_T(f["vselTb"]),
                     _channel_major(cache_win_k), _channel_major(cache_win_v),
                     rows_nat(f["kwinb"]), rows_T(f["vwinTb"]), tqs, 4)
    o = oT.reshape(Bs, N_KV, HEAD_DIM, GROUP, tqs)[..., :Ts]
    o = jnp.transpose(o, (0, 4, 1, 3, 2)).reshape(1, n_tok, Q_W).astype(BF16)
    y = _back(x.reshape(1, n_tok, D_MODEL), o, u_out.reshape(1, n_tok, D_MODEL), f["gmg"], mod_tok, prm, n_tok)
    hs = lambda a: a[0].T.reshape(Bs, Ts, N_KV, HEAD_DIM)
    buf = cache_win_k.shape[1]
    keep = min(WINDOW, buf + Ts)
    kw = jnp.concatenate([cache_win_k, hs(f["kwinT"])], axis=1)[:, -keep:]
    vw = jnp.concatenate([cache_win_v, hs(f["vwinT"])], axis=1)[:, -keep:]
    return y.reshape(Bs, Ts, D_MODEL), (hs(f["kcmpT"]), hs(f["vcmpT"]), hs(f["kselT"]), hs(f["vselT"]), kw, vw, conv_state)


def kernel(x_prompt, x_sample, cache_cmp_k, cache_cmp_v, cache_sel_k, cache_sel_v, cache_win_k, cache_win_v,
           state_conv, page_table, c_prompt, c_sample, w_ada, b_ada, mix_norm_g, w_in, q_norm_g, k_norm_g,
           w_dw, b_dw, conv_ln_g, conv_ln_b, w_pw2, cmp_mod_k, cmp_w_k, cmp_mod_v, cmp_w_v, w_nsa_o, w_out,
           ffn_norm_g, w_gate, w_up, w_down):
    depth = w_in.shape[0]
    yp, ys = x_prompt, x_sample
    st_p, st_s = [], []
    for l in range(depth):
        p = dict(w_ada=w_ada[l], b_ada=b_ada[l], mix_norm_g=mix_norm_g[l], w_in=w_in[l],
                 q_norm_g=q_norm_g[l], k_norm_g=k_norm_g[l], w_dw=w_dw[l], b_dw=b_dw[l],
                 conv_ln_g=conv_ln_g[l], conv_ln_b=conv_ln_b[l], w_pw2=w_pw2[l],
                 cmp_mod_k=cmp_mod_k[l], cmp_w_k=cmp_w_k[l], cmp_mod_v=cmp_mod_v[l],
                 cmp_w_v=cmp_w_v[l], w_nsa_o=w_nsa_o[l], w_out=w_out[l],
                 ffn_norm_g=ffn_norm_g[l], w_gate=w_gate[l], w_up=w_up[l], w_down=w_down[l])
        prm = _prep_params(p)
        nb_p = c_prompt.shape[0]
        mod = _ada(jnp.concatenate([c_prompt, c_sample], axis=0), p["w_ada"], p["b_ada"])
        yp, sp = _prompt_layer(yp, mod[:nb_p], prm, 256, 128)
        ys, ss = _sample_layer(ys, mod[nb_p:], cache_cmp_k[l], cache_cmp_v[l], cache_sel_k[l], cache_sel_v[l],
                               cache_win_k[l], cache_win_v[l], state_conv[l], page_table, prm)
        st_p.append(sp)
        st_s.append(ss)
    stk = lambda lst, j: jnp.stack([s[j] for s in lst])
    return (yp, ys) + tuple(stk(st_p, j) for j in range(7)) + tuple(stk(st_s, j) for j in range(7))
```

```python
import functools
import math

import jax
import jax.numpy as jnp
from jax import lax
from jax.experimental import pallas as pl
from jax.experimental.pallas import tpu as pltpu

F32 = jnp.float32
BF16 = jnp.bfloat16

D_MODEL = 1024
D_CONV = 512
CONV_K = 31
N_HEADS = 8
N_KV = 2
HEAD_DIM = 64
GROUP = 4
Q_W = N_HEADS * HEAD_DIM
KV_W = N_KV * HEAD_DIM
BLOCK = 64
N_SEL = 16
WINDOW = 512
ROPE_THETA = 10000.0
D_FF = 2816
EPS = 1e-6
FORCED_BONUS = 2.0 * GROUP
PAGE = 128
OFF_Q = 2 * D_CONV
OFF_KV = OFF_Q + Q_W
OFF_NG = OFF_KV + 6 * KV_W
OFF_MG = OFF_NG + 3 * N_HEADS
Q_SCALE = HEAD_DIM ** -0.5 * math.log2(math.e)
NEG = -(2.0 ** 100)
M_INIT = -(2.0 ** 99)
ONES_ROWS = 16
V_AUG = HEAD_DIM + ONES_ROWS
K_AUG = KV_W + 16
N_T_ROWS = 1312
CONV_HALO = 32
SUBLANES = 8
LANES = 128
SEL_CHUNK = 512
VMEM_LIMIT = 56 * 1024 * 1024


def _cparams(sem):
    return pltpu.CompilerParams(dimension_semantics=sem, vmem_limit_bytes=VMEM_LIMIT)


def _ada_kernel(c_ref, w_ref, b_ref, o_ref):
    c = c_ref[...]
    s = c * jax.nn.sigmoid(c)
    o_ref[...] = jnp.dot(s.astype(BF16), w_ref[...].astype(BF16), preferred_element_type=F32) + b_ref[...]


def _ada(c, w_ada, b_ada):
    n, tn = c.shape[0], 1024
    return pl.pallas_call(
        _ada_kernel,
        grid=(w_ada.shape[1] // tn,),
        in_specs=[pl.BlockSpec((n, D_MODEL), lambda j: (0, 0)),
                  pl.BlockSpec((D_MODEL, tn), lambda j: (0, j)),
                  pl.BlockSpec((1, tn), lambda j: (0, j))],
        out_specs=pl.BlockSpec((n, tn), lambda j: (0, j)),
        out_shape=jax.ShapeDtypeStruct((n, w_ada.shape[1]), F32),
        compiler_params=_cparams(("arbitrary",)),
        name="ada",
    )(c, w_ada, b_ada.reshape(1, -1))


def _front_kernel(x_ref, sh1_ref, sc1_ref, g_ref, wn_ref, wt_ref, gq_ref, gk_ref, cos_ref, sin_ref,
                  u_ref, gmg_ref, qp_ref, qr_ref, kcmpT_ref, vcmpT_ref, kselT_ref, vselT_ref, kwinT_ref, vwinT_ref,
                  kselb_ref, kwinb_ref, vselTb_ref, vwinTb_ref, ngT_ref):
    x = x_ref[0]
    ms = jnp.mean(x * x, axis=-1, keepdims=True)
    h = x * lax.rsqrt(ms + EPS) * g_ref[...]
    h = h * (1.0 + sc1_ref[0]) + sh1_ref[0]
    hb = h.astype(BF16)
    zn = jnp.dot(hb, wn_ref[...], preferred_element_type=F32)
    u_ref[0] = zn[:, :D_CONV] * jax.nn.sigmoid(zn[:, D_CONV:2 * D_CONV])
    gmg_ref[0] = jax.nn.sigmoid(zn[:, 2 * D_CONV:])
    zt = lax.dot_general(wt_ref[...], hb, (((1,), (1,)), ((), ())), preferred_element_type=F32)
    cos = cos_ref[...]
    sin = sin_ref[...]
    half = HEAD_DIM // 2

    def norm(v, g):
        m = jnp.mean(v * v, axis=0, keepdims=True)
        return v * lax.rsqrt(m + EPS) * g

    def rope(v):
        a, b = v[:half], v[half:]
        return jnp.concatenate([a * cos - b * sin, b * cos + a * sin], axis=0)

    gq = gq_ref[...]
    for hh in range(N_HEADS):
        q = norm(zt[hh * HEAD_DIM:(hh + 1) * HEAD_DIM], gq) * Q_SCALE
        qp_ref[0, hh * HEAD_DIM:(hh + 1) * HEAD_DIM, :] = q.astype(BF16)
        qr_ref[0, hh * HEAD_DIM:(hh + 1) * HEAD_DIM, :] = rope(q).astype(BF16)

    def head_pair(j, g=None, rotary=False):
        outs = []
        for hh in range(N_KV):
            lo = Q_W + j * KV_W + hh * HEAD_DIM
            v = zt[lo:lo + HEAD_DIM]
            if g is not None:
                v = norm(v, g)
            if rotary:
                v = rope(v)
            outs.append(v)
        return jnp.concatenate(outs, axis=0)

    k_sel = head_pair(2, gk_ref[1], True)
    v_sel = head_pair(3)
    k_win = head_pair(4, gk_ref[2], True)
    v_win = head_pair(5)
    kcmpT_ref[0] = head_pair(0, gk_ref[0])
    vcmpT_ref[0] = head_pair(1)
    kselT_ref[0] = k_sel
    vselT_ref[0] = v_sel
    kwinT_ref[0] = k_win
    vwinT_ref[0] = v_win
    tm = x.shape[0]
    blk = lax.shift_right_logical(pl.program_id(1) * tm + lax.broadcasted_iota(jnp.int32, (tm, K_AUG - KV_W), 0), 6)
    onehot = (blk % (SEL_CHUNK // BLOCK)) == lax.broadcasted_iota(jnp.int32, (tm, K_AUG - KV_W), 1)
    kselb_ref[0] = jnp.concatenate([k_sel.T, jnp.where(onehot, 1.0, 0.0)], axis=1).astype(BF16)
    kwinb_ref[0] = k_win.T.astype(BF16)
    ones = jnp.ones((ONES_ROWS, tm), F32)

    def v_aug(v):
        return jnp.concatenate([v[:HEAD_DIM], ones, v[HEAD_DIM:], ones], axis=0).astype(BF16)

    vselTb_ref[0] = v_aug(v_sel)
    vwinTb_ref[0] = v_aug(v_win)
    ngT_ref[0] = jax.nn.sigmoid(zt[Q_W + 6 * KV_W:])


def _front(x, mod, pos, prm, tm):
    B, T, _ = x.shape
    Tm = mod.shape[1]
    tmm = 1 if Tm == 1 else tm
    half = HEAD_DIM // 2
    inv = jnp.power(ROPE_THETA, -jnp.arange(half, dtype=F32) / half)
    ang = pos.astype(F32)[:, None] * inv[None, :]
    cosT, sinT = jnp.cos(ang).T, jnp.sin(ang).T
    gq = jnp.broadcast_to(prm["q_norm_g"][:, None], (HEAD_DIM, tm))
    gk = jnp.broadcast_to(prm["k_norm_g"][:, :, None], (3, HEAD_DIM, tm))
    nt = T // tm
    tok = lambda w: pl.BlockSpec((1, tm, w), lambda b, t: (b, t, 0))
    tokT = lambda r: pl.BlockSpec((1, r, tm), lambda b, t: (b, 0, t))
    modspec = lambda k: pl.BlockSpec((1, tmm, D_MODEL), (lambda b, t: (b, t, k)) if Tm != 1 else (lambda b, t: (b, 0, k)))
    const = lambda shape: pl.BlockSpec(shape, lambda b, t: (0,) * len(shape))
    sd = jax.ShapeDtypeStruct
    outs = pl.pallas_call(
        _front_kernel,
        grid=(B, nt),
        in_specs=[tok(D_MODEL), modspec(0), modspec(1), const((1, D_MODEL)),
                  const((D_MODEL, 3 * D_MODEL)), const((N_T_ROWS, D_MODEL)),
                  const((HEAD_DIM, tm)), const((3, HEAD_DIM, tm)),
                  pl.BlockSpec((half, tm), lambda b, t: (0, t)), pl.BlockSpec((half, tm), lambda b, t: (0, t))],
        out_specs=[tok(D_CONV), tok(2 * D_MODEL), tokT(Q_W), tokT(Q_W)] + [tokT(KV_W)] * 6
                  + [tok(K_AUG), tok(KV_W), tokT(N_KV * V_AUG), tokT(N_KV * V_AUG), tokT(32)],
        out_shape=[sd((B, T, D_CONV), F32), sd((B, T, 2 * D_MODEL), F32), sd((B, Q_W, T), BF16), sd((B, Q_W, T), BF16)]
                  + [sd((B, KV_W, T), F32)] * 6
                  + [sd((B, T, K_AUG), BF16), sd((B, T, KV_W), BF16), sd((B, N_KV * V_AUG, T), BF16),
                     sd((B, N_KV * V_AUG, T), BF16), sd((B, 32, T), F32)],
        compiler_params=_cparams(("parallel", "arbitrary")),
        name="front",
    )(x, mod, mod, prm["mix_norm_g"].reshape(1, -1), prm["wn"], prm["wt"], gq, gk, cosT, sinT)
    names = ("u", "gmg", "qp", "qr", "kcmpT", "vcmpT", "kselT", "vselT", "kwinT", "vwinT",
             "kselb", "kwinb", "vselTb", "vwinTb", "ngT")
    return dict(zip(names, outs))


def _conv_kernel(u_ref, st_ref, wdw_ref, bdw_ref, lng_ref, lnb_ref, wpw_ref, uo_ref, so_ref, ext_ref, sh_ref, *, tm):
    t = pl.program_id(1)
    pad = CONV_HALO - (CONV_K - 1)
    n_sh = tm + CONV_HALO - SUBLANES

    @pl.when(t == 0)
    def _():
        ext_ref[0:pad, :] = jnp.zeros((pad, D_CONV), F32)
        ext_ref[pad:CONV_HALO, :] = st_ref[0]

    ext_ref[CONV_HALO:CONV_HALO + tm, :] = u_ref[0]
    for r in range(1, SUBLANES):
        sh_ref[r - 1] = ext_ref[r:r + n_sh, :]
    y = jnp.zeros((tm, D_CONV), F32) + bdw_ref[...]
    for k in range(CONV_K):
        a, r = divmod(pad + k, SUBLANES)
        rows = ext_ref[SUBLANES * a:SUBLANES * a + tm, :] if r == 0 else sh_ref[r - 1, SUBLANES * a:SUBLANES * a + tm, :]
        y = y + rows * wdw_ref[k:k + 1, :]
    mu = jnp.mean(y, axis=-1, keepdims=True)
    yc = y - mu
    var = jnp.mean(yc * yc, axis=-1, keepdims=True)
    yn = yc * lax.rsqrt(var + EPS) * lng_ref[...] + lnb_ref[...]
    ya = yn * jax.nn.sigmoid(yn)
    uo_ref[0] = jnp.dot(ya.astype(BF16), wpw_ref[...], preferred_element_type=F32)
    so_ref[0] = ext_ref[tm + pad:tm + CONV_HALO, :]
    carry = ext_ref[tm:tm + CONV_HALO, :]
    ext_ref[0:CONV_HALO, :] = carry


def _conv(u, state, prm, tm):
    B, T, _ = u.shape
    const = lambda shape: pl.BlockSpec(shape, lambda b, t: (0,) * len(shape))
    return pl.pallas_call(
        functools.partial(_conv_kernel, tm=tm),
        grid=(B, T // tm),
        in_specs=[pl.BlockSpec((1, tm, D_CONV), lambda b, t: (b, t, 0)),
                  pl.BlockSpec((1, CONV_K - 1, D_CONV), lambda b, t: (b, 0, 0)),
                  const((CONV_K, D_CONV)), const((1, D_CONV)), const((1, D_CONV)), const((1, D_CONV)),
                  const((D_CONV, D_MODEL))],
        out_specs=[pl.BlockSpec((1, tm, D_MODEL), lambda b, t: (b, t, 0)),
                   pl.BlockSpec((1, CONV_K - 1, D_CONV), lambda b, t: (b, 0, 0))],
        out_shape=[jax.ShapeDtypeStruct((B, T, D_MODEL), F32), jax.ShapeDtypeStruct((B, CONV_K - 1, D_CONV), F32)],
        scratch_shapes=[pltpu.VMEM((CONV_HALO + tm, D_CONV), F32),
                        pltpu.VMEM((SUBLANES - 1, tm + CONV_HALO - SUBLANES, D_CONV), F32)],
        compiler_params=_cparams(("parallel", "arbitrary")),
        name="conv",
    )(u, state, prm["w_dw"], prm["b_dw"].reshape(1, -1), prm["conv_ln_g"].reshape(1, -1),
      prm["conv_ln_b"].reshape(1, -1), prm["w_pw2b"])


def _compress_kernel(*refs, n_in, n_prefetch, group):
    refs = refs[n_prefetch:]
    k_refs, v_refs = refs[:n_in], refs[n_in:2 * n_in]
    modk_ref, wk_ref, modv_ref, wv_ref, s_ref, kc_ref, vc_ref, acc_ref = refs[2 * n_in:]
    j = pl.program_id(1)
    slot = j % group

    @pl.when(slot == 0)
    def _():
        acc_ref[...] = jnp.zeros_like(acc_ref)

    place = s_ref[slot]

    def modulated(x_refs, mod_ref):
        xT = jnp.concatenate([r[0] for r in x_refs], axis=1) if n_in > 1 else x_refs[0][0]
        return (xT * (1.0 + mod_ref[...])).astype(BF16)

    sums = jnp.dot(jnp.concatenate([modulated(k_refs, modk_ref), modulated(v_refs, modv_ref)], axis=0), place,
                   preferred_element_type=F32)
    for i, (w_ref, out_ref) in enumerate(((wk_ref, kc_ref), (wv_ref, vc_ref))):
        acc_ref[i] += sums[i * KV_W:(i + 1) * KV_W]
        out_ref[0] = jnp.dot(w_ref[...], acc_ref[i].astype(BF16), preferred_element_type=F32)


def _compress_consts(prm, cols):
    nbs = max(cols // BLOCK, 1)
    group = LANES // nbs
    modT = lambda m: jnp.tile(m.T, (N_KV, max(cols // BLOCK, 1)))[:, :cols]
    wT = lambda w: jnp.kron(jnp.eye(N_KV, dtype=F32), w.T).astype(BF16)
    r = jnp.arange(cols)[None, :, None]
    g = jnp.arange(group)[:, None, None]
    lane = jnp.arange(LANES)[None, None, :]
    place = jnp.where(lane == g * nbs + r // BLOCK, 1.0 / BLOCK, 0.0).astype(BF16)
    return (modT(prm["cmp_mod_k"]), wT(prm["cmp_w_k"]), modT(prm["cmp_mod_v"]), wT(prm["cmp_w_v"]), place), group


def _compress_call(kT, vT, page_table, prm, cols, n_in, x_spec, grid, n_out_blocks):
    consts, group = _compress_consts(prm, cols * n_in)
    n_prefetch = 0 if page_table is None else 1
    const = lambda shape: pl.BlockSpec(shape, lambda *a: (0,) * len(shape))
    ospec = pl.BlockSpec((1, KV_W, LANES), lambda b, j, *a: (b, 0, j // group))
    osh = jax.ShapeDtypeStruct((grid[0], KV_W, n_out_blocks * LANES), F32)
    in_specs = ([x_spec(i) for i in range(n_in)] * 2
                + [const((KV_W, cols * n_in)), const((KV_W, KV_W)), const((KV_W, cols * n_in)), const((KV_W, KV_W)),
                   const((group, cols * n_in, LANES))])
    scratch = [pltpu.VMEM((2, KV_W, LANES), F32)]
    kern = functools.partial(_compress_kernel, n_in=n_in, n_prefetch=n_prefetch, group=group)
    if page_table is None:
        call = pl.pallas_call(kern, grid=grid, in_specs=in_specs, out_specs=[ospec, ospec], out_shape=[osh, osh],
                              scratch_shapes=scratch, compiler_params=_cparams(("parallel", "arbitrary")), name="compress")
        return call(*([kT] * n_in), *([vT] * n_in), *consts)
    grid_spec = pltpu.PrefetchScalarGridSpec(num_scalar_prefetch=1, grid=grid, in_specs=in_specs,
                                             out_specs=[ospec, ospec], scratch_shapes=scratch)
    call = pl.pallas_call(kern, grid_spec=grid_spec, out_shape=[osh, osh],
                          compiler_params=_cparams(("parallel", "arbitrary")), name="compress_paged")
    return call(page_table.reshape(-1), *([kT] * n_in), *([vT] * n_in), *consts)


def _compress(kT, vT, prm, cols):
    B, _, L = kT.shape
    cols = min(cols, L)
    n_blocks = -(-(L // BLOCK) // LANES) if L >= BLOCK else 1
    x_spec = lambda i: pl.BlockSpec((1, KV_W, cols), lambda b, j: (b, 0, j))
    return _compress_call(kT, vT, None, prm, cols, 1, x_spec, (B, L // cols), n_blocks)


def _compress_paged(cache_kT, cache_vT, page_table, prm, pages_per_step):
    B, n_pages = page_table.shape
    x_spec = lambda i: pl.BlockSpec((1, KV_W, PAGE), lambda b, j, pt: (pt[b * n_pages + j * pages_per_step + i], 0, 0))
    n_blocks = -(-(n_pages * PAGE // BLOCK) // LANES)
    return _compress_call(cache_kT, cache_vT, page_table, prm, PAGE, pages_per_step, x_spec,
                          (B, n_pages // pages_per_step), n_blocks)


def _attend_scores(s, vT, mask, carry):
    m, acc = carry
    if mask is not None:
        s = jnp.where(mask, s, NEG)
    m_new = jnp.maximum(m, jnp.max(s, axis=0, keepdims=True))
    alpha = jnp.exp2(m - m_new)
    p = jnp.exp2(s - m_new).astype(BF16)
    return m_new, alpha * acc + jnp.dot(vT, p, preferred_element_type=F32)


def _attend(qT, k, vT, mask, carry):
    return _attend_scores(jnp.dot(k, qT, preferred_element_type=F32), vT, mask, carry)


def _attend_init(L):
    return jnp.full((1, L), M_INIT, F32), jnp.zeros((V_AUG, L), F32)


def _attend_out(carry):
    _, acc = carry
    return acc[:HEAD_DIM] / jnp.maximum(acc[HEAD_DIM:HEAD_DIM + 1], 1e-30)


def _with_ones(vT):
    return jnp.concatenate([vT, jnp.ones((ONES_ROWS, vT.shape[1]), BF16)], axis=0)


def _block_mask(sel_rows, kc):
    nb, L = sel_rows.shape
    return jnp.broadcast_to((sel_rows > 0.5)[:, None, :], (nb, BLOCK, L)).reshape(kc, L)


def _cmp_and_select(qpT, kc, vcT, qpos, tq):
    nb, L = kc.shape[0], qpT.shape[1]
    s = jnp.dot(kc, qpT, preferred_element_type=F32)
    blk = lax.broadcasted_iota(jnp.int32, (nb, L), 0)
    ok = (blk + 1) * BLOCK <= qpos + 1
    s = jnp.where(ok, s, -jnp.inf)
    m = jnp.max(s, axis=0, keepdims=True)
    m = jnp.where(m > -jnp.inf, m, 0.0)
    e = jnp.where(ok, jnp.exp2(s - m), 0.0)
    p = e / jnp.maximum(jnp.sum(e, axis=0, keepdims=True), 1e-30)
    o_cmp = jnp.dot(vcT, p.astype(BF16), preferred_element_type=F32)
    if tq % LANES == 0:
        w = tq
        imp = p[:, :tq]
        for g in range(1, GROUP):
            imp = imp + p[:, g * tq:(g + 1) * tq]
    else:
        w = L
        imp = p
        for g in range(1, GROUP):
            imp = imp + pltpu.roll(p, g * tq, 1)
    blk, qpos = blk[:, :w], qpos[:, :w]
    cur = lax.shift_right_logical(qpos, 6)
    visible = blk <= cur
    forced = (blk == 0) | (blk == cur) | (blk == cur - 1)
    score = jnp.where(visible, imp + jnp.where(forced, FORCED_BONUS, 0.0), -1.0)
    ridx = blk.astype(F32)
    sel = jnp.zeros_like(score)
    rem = score
    for _ in range(min(N_SEL, nb)):
        mx = jnp.max(rem, axis=0, keepdims=True)
        first = jnp.min(jnp.where(rem == mx, ridx, float(nb)), axis=0, keepdims=True)
        pick = ridx == first
        rem = jnp.where(pick, -jnp.inf, rem)
        sel = jnp.where(pick, 1.0, sel)
    if w != L:
        sel = jnp.concatenate([sel] * GROUP, axis=1)
    return o_cmp, sel


def _nsa_prompt_kernel(qp_ref, qr_ref, kc_ref, vcT_ref, ksel_ref, vselT_ref, kwin_ref, vwinT_ref, ngT_ref,
                       o_ref, bias_ref, s_ref, *, tq):
    t = pl.program_id(1)
    s0 = t * tq
    L = GROUP * tq
    lane = lax.broadcasted_iota(jnp.int32, (1, L), 1)
    qpos = s0 + lane % tq
    ng = ngT_ref[0]
    heads = [slice(h * HEAD_DIM, (h + 1) * HEAD_DIM) for h in range(N_KV)]

    def qpad(ref, h):
        q = jnp.concatenate([ref[0, (h * GROUP + g) * HEAD_DIM:(h * GROUP + g + 1) * HEAD_DIM, :]
                             for g in range(GROUP)], axis=1)
        z = jnp.zeros_like(q)
        return jnp.concatenate([q, z] if h == 0 else [z, q], axis=0)

    qr = [qpad(qr_ref, h) for h in range(N_KV)]
    vrows = [slice(h * V_AUG, (h + 1) * V_AUG) for h in range(N_KV)]
    n_blk = SEL_CHUNK // BLOCK
    o_cmp = []
    for h in range(N_KV):
        o, sel = _cmp_and_select(qpad(qp_ref, h), kc_ref[0], vcT_ref[0, heads[h], :], qpos, tq)
        o_cmp.append(o)
        bias = jnp.where(sel > 0.5, 0.0, NEG).reshape(-1, n_blk, L)
        bias = jnp.concatenate([bias, jnp.full((1, n_blk, L), NEG, F32)], axis=0)
        bias_ref[h] = jnp.concatenate([bias, jnp.zeros_like(bias)], axis=1).astype(BF16)

    n_full = s0 // SEL_CHUNK
    dead = bias_ref.shape[1] - 1

    def scores(c, bias_idx, h):
        k0 = pl.multiple_of(c * SEL_CHUNK, SEL_CHUNK)
        return jnp.dot(ksel_ref[0, pl.ds(k0, SEL_CHUNK), :], jnp.concatenate([qr[h], bias_ref[h, bias_idx]], axis=0),
                       preferred_element_type=F32)

    def issue(c, slot):
        c_eff = jnp.minimum(c, jnp.maximum(n_full - 1, 0))
        for h in range(N_KV):
            s_ref[slot, h] = scores(c_eff, jnp.where(c < n_full, c, dead), h)

    def consume(c, slot, carries):
        k0 = pl.multiple_of(jnp.minimum(c, jnp.maximum(n_full - 1, 0)) * SEL_CHUNK, SEL_CHUNK)
        return tuple(_attend_scores(s_ref[slot, h], vselT_ref[0, vrows[h], pl.ds(k0, SEL_CHUNK)], None, carries[h])
                     for h in range(N_KV))

    issue(0, 0)

    def pair(i, carries):
        c = 2 * i
        issue(c + 1, 1)
        carries = consume(c, 0, carries)
        issue(c + 2, 0)
        return consume(c + 1, 1, carries)

    carries = lax.fori_loop(0, (n_full + 1) // 2, pair, (_attend_init(L),) * N_KV)
    k0 = pl.multiple_of(n_full * SEL_CHUNK, SEL_CHUNK)
    causal = k0 + lax.broadcasted_iota(jnp.int32, (SEL_CHUNK, L), 0) <= qpos
    diag = [scores(n_full, n_full, h) for h in range(N_KV)]
    carries = tuple(_attend_scores(diag[h], vselT_ref[0, vrows[h], pl.ds(k0, SEL_CHUNK)], causal, carries[h])
                    for h in range(N_KV))

    w0 = pl.multiple_of(jnp.maximum(s0 - WINDOW, 0), tq)
    kpos = w0 + lax.broadcasted_iota(jnp.int32, (WINDOW + tq, L), 0)
    wmask = (kpos <= qpos) & (kpos > qpos - WINDOW)
    kw = kwin_ref[0, pl.ds(w0, WINDOW + tq), :]

    def gate(h, br):
        return jnp.concatenate([ng[h * 12 + g * 3 + br:h * 12 + g * 3 + br + 1, :] for g in range(GROUP)], axis=1)

    for h in range(N_KV):
        o_sel = _attend_out(carries[h])
        o_win = _attend_out(_attend(qr[h], kw, vwinT_ref[0, vrows[h], pl.ds(w0, WINDOW + tq)], wmask, _attend_init(L)))
        o = gate(h, 0) * o_cmp[h] + gate(h, 1) * o_sel + gate(h, 2) * o_win
        oT = jnp.concatenate([o[:, g * tq:(g + 1) * tq] for g in range(GROUP)], axis=0)
        o_ref[0, :, h * GROUP * HEAD_DIM:(h + 1) * GROUP * HEAD_DIM] = oT.T.astype(o_ref.dtype)


def _nsa_prompt(f, kcb, vcTb, tq):
    B, _, T = f["qp"].shape
    assert T % SEL_CHUNK == 0 and SEL_CHUNK % tq == 0 and WINDOW % tq == 0 and T >= WINDOW + tq
    nbp = kcb.shape[1]
    full = lambda shape: pl.BlockSpec((1,) + shape, lambda b, t: (b, 0, 0))
    return pl.pallas_call(
        functools.partial(_nsa_prompt_kernel, tq=tq),
        grid=(B, T // tq),
        in_specs=[pl.BlockSpec((1, Q_W, tq), lambda b, t: (b, 0, t)), pl.BlockSpec((1, Q_W, tq), lambda b, t: (b, 0, t)),
                  full((nbp, KV_W)), full((KV_W, nbp)),
                  full((T, K_AUG)), full((N_KV * V_AUG, T)), full((T, KV_W)), full((N_KV * V_AUG, T)),
                  pl.BlockSpec((1, 32, tq), lambda b, t: (b, 0, t))],
        out_specs=pl.BlockSpec((1, tq, Q_W), lambda b, t: (b, t, 0)),
        out_shape=jax.ShapeDtypeStruct((B, T, Q_W), BF16),
        scratch_shapes=[pltpu.VMEM((N_KV, nbp // (SEL_CHUNK // BLOCK) + 1, 2 * (SEL_CHUNK // BLOCK), GROUP * tq), BF16),
                        pltpu.VMEM((2, N_KV, SEL_CHUNK, GROUP * tq), F32)],
        compiler_params=_cparams(("parallel", "arbitrary")),
        name="nsa_prompt",
    )(f["qp"], f["qr"], kcb, vcTb, f["kselb"], f["vselTb"], f["kwinb"], f["vwinTb"], f["ngT"])


def _nsa_sample_kernel(pt_ref, qp_ref, qr_ref, kc_ref, vcT_ref, gate_ref, *refs, tqs, past_len, pages_per_step):
    del pt_ref
    kpages, vpages = refs[:pages_per_step], refs[pages_per_step:2 * pages_per_step]
    (knew_ref, vnewT_ref, cwkT_ref, cwvT_ref, kwnew_ref, vwnewT_ref,
     o_ref, sel_ref, ocmp_ref, m_ref, acc_ref) = refs[2 * pages_per_step:]
    j = pl.program_id(1)
    n_steps = pl.num_programs(1)
    L = GROUP * tqs
    kc_step = pages_per_step * PAGE
    lane = lax.broadcasted_iota(jnp.int32, (1, L), 1)
    qi = lane % tqs
    qpos = past_len + qi
    heads = [slice(h * HEAD_DIM, (h + 1) * HEAD_DIM) for h in range(N_KV)]

    @pl.when(j == 0)
    def _():
        for h in range(N_KV):
            o_cmp, sel = _cmp_and_select(qp_ref[0, h], kc_ref[0], vcT_ref[0, heads[h], :], qpos, tqs)
            sel_ref[h] = sel
            ocmp_ref[h] = o_cmp
            m_ref[h], acc_ref[h] = _attend_init(L)

    k = jnp.concatenate([r[0] for r in kpages], axis=1).T.astype(BF16)
    vT = jnp.concatenate([r[0] for r in vpages], axis=1).astype(BF16)
    r0 = pl.multiple_of(j * (kc_step // BLOCK), kc_step // BLOCK)
    scores = [jnp.dot(k, qr_ref[0, h], preferred_element_type=F32) for h in range(N_KV)]
    for h in range(N_KV):
        mask = _block_mask(sel_ref[h, pl.ds(r0, kc_step // BLOCK), :], kc_step)
        m_ref[h], acc_ref[h] = _attend_scores(scores[h], _with_ones(vT[heads[h]]), mask, (m_ref[h], acc_ref[h]))

    @pl.when(j == n_steps - 1)
    def _():
        nb_past = past_len // BLOCK
        buf = cwkT_ref.shape[2]
        r_new = lax.broadcasted_iota(jnp.int32, (tqs, L), 0)
        r_buf = lax.broadcasted_iota(jnp.int32, (buf, L), 0)
        cwk = cwkT_ref[0].T.astype(BF16)
        cwvT = cwvT_ref[0].astype(BF16)
        gates = gate_ref[0]
        for h in range(N_KV):
            qr = qr_ref[0, h]
            mask = (sel_ref[h, nb_past:nb_past + 1, :] > 0.5) & (r_new <= qi)
            carry = _attend(qr, knew_ref[0], _with_ones(vnewT_ref[0, heads[h], :]), mask, (m_ref[h], acc_ref[h]))
            o_sel = _attend_out(carry)
            carry = _attend(qr, cwk, _with_ones(cwvT[heads[h]]), r_buf > qi + (buf - WINDOW), _attend_init(L))
            carry = _attend(qr, kwnew_ref[0], _with_ones(vwnewT_ref[0, heads[h], :]), r_new <= qi, carry)
            o_win = _attend_out(carry)
            g = gates[h]
            o_ref[0, h] = g[0:1] * ocmp_ref[h] + g[1:2] * o_sel + g[2:3] * o_win


def _nsa_sample(qpT, qrT, kcb, vcTb, gates, cache_sel_kT, cache_sel_vT, page_table, knew, vnewT,
                cache_win_kT, cache_win_vT, kwnew, vwnewT, tqs, pages_per_step):
    Bs, n_pages = page_table.shape
    L = GROUP * tqs
    nbp = kcb.shape[1]
    buf = cache_win_kT.shape[2]
    per_b = lambda shape: pl.BlockSpec((1,) + shape, lambda b, j, pt: (b,) + (0,) * len(shape))
    pspec = lambda i: pl.BlockSpec((1, KV_W, PAGE), lambda b, j, pt: (pt[b * n_pages + j * pages_per_step + i], 0, 0))
    grid_spec = pltpu.PrefetchScalarGridSpec(
        num_scalar_prefetch=1,
        grid=(Bs, n_pages // pages_per_step),
        in_specs=[per_b((N_KV, KV_W, L)), per_b((N_KV, KV_W, L)), per_b((nbp, KV_W)), per_b((KV_W, nbp)),
                  per_b((N_KV, 8, L))]
                 + [pspec(i) for i in range(pages_per_step)] * 2
                 + [per_b((tqs, KV_W)), per_b((KV_W, tqs)), per_b((KV_W, buf)), per_b((KV_W, buf)),
                    per_b((tqs, KV_W)), per_b((KV_W, tqs))],
        out_specs=per_b((N_KV, HEAD_DIM, L)),
        scratch_shapes=[pltpu.VMEM((N_KV, nbp, L), F32), pltpu.VMEM((N_KV, HEAD_DIM, L), F32),
                        pltpu.VMEM((N_KV, 1, L), F32), pltpu.VMEM((N_KV, V_AUG, L), F32)],
    )
    return pl.pallas_call(
        functools.partial(_nsa_sample_kernel, tqs=tqs, past_len=n_pages * PAGE, pages_per_step=pages_per_step),
        grid_spec=grid_spec,
        out_shape=jax.ShapeDtypeStruct((Bs, N_KV, HEAD_DIM, L), F32),
        compiler_params=_cparams(("parallel", "arbitrary")),
        name="nsa_sample",
    )(page_table.reshape(-1), qpT, qrT, kcb, vcTb, gates,
      *([cache_sel_kT] * pages_per_step), *([cache_sel_vT] * pages_per_step),
      knew, vnewT, cache_win_kT, cache_win_vT, kwnew, vwnewT)


def _back_kernel(x_ref, o_ref, uo_ref, gmg_ref, g1_ref, sh2_ref, sc2_ref, g2_ref, fg_ref,
                 wno_ref, wout_ref, wg_ref, wu_ref, wd_ref, y_ref):
    nsa = jnp.dot(o_ref[0], wno_ref[...], preferred_element_type=F32)
    gmg = gmg_ref[0]
    merged = gmg[:, :D_MODEL] * uo_ref[0] + gmg[:, D_MODEL:] * nsa
    mix = jnp.dot(merged.astype(BF16), wout_ref[...], preferred_element_type=F32)
    x1 = x_ref[0] + g1_ref[0] * mix
    ms = jnp.mean(x1 * x1, axis=-1, keepdims=True)
    h = x1 * lax.rsqrt(ms + EPS) * fg_ref[...]
    h = h * (1.0 + sc2_ref[0]) + sh2_ref[0]
    hb = h.astype(BF16)
    a = jnp.dot(hb, wg_ref[...], preferred_element_type=F32)
    b = jnp.dot(hb, wu_ref[...], preferred_element_type=F32)
    act = a * jax.nn.sigmoid(a) * b
    ffn = jnp.dot(act.astype(BF16), wd_ref[...], preferred_element_type=F32)
    y_ref[0] = x1 + g2_ref[0] * ffn


def _back(x, o, u_out, gmg, mod, prm, tm):
    B, T, _ = x.shape
    Tm = mod.shape[1]
    tmm = 1 if Tm == 1 else tm
    tok = lambda w: pl.BlockSpec((1, tm, w), lambda b, t: (b, t, 0))
    modspec = lambda k: pl.BlockSpec((1, tmm, D_MODEL), (lambda b, t: (b, t, k)) if Tm != 1 else (lambda b, t: (b, 0, k)))
    wspec = lambda shape: pl.BlockSpec(shape, lambda b, t: (0, 0), pipeline_mode=pl.Buffered(1))
    return pl.pallas_call(
        _back_kernel,
        grid=(B, T // tm),
        in_specs=[tok(D_MODEL), tok(Q_W), tok(D_MODEL), tok(2 * D_MODEL),
                  modspec(2), modspec(3), modspec(4), modspec(5), wspec((1, D_MODEL)),
                  wspec((Q_W, D_MODEL)), wspec((D_MODEL, D_MODEL)), wspec((D_MODEL, D_FF)), wspec((D_MODEL, D_FF)),
                  wspec((D_FF, D_MODEL))],
        out_specs=tok(D_MODEL),
        out_shape=jax.ShapeDtypeStruct((B, T, D_MODEL), F32),
        compiler_params=_cparams(("parallel", "arbitrary")),
        name="back",
    )(x, o, u_out, gmg, mod, mod, mod, mod, prm["ffn_norm_g"].reshape(1, -1),
      prm["w_nsa_ob"], prm["w_outb"], prm["w_gateb"], prm["w_upb"], prm["w_downb"])


def _prep_params(p):
    w_in = p["w_in"]
    q = dict(p)
    q["wn"] = jnp.concatenate([w_in[:, :OFF_Q], w_in[:, OFF_MG:]], axis=1).astype(BF16)
    wt = jnp.concatenate([w_in[:, OFF_Q:OFF_MG], jnp.zeros((D_MODEL, N_T_ROWS - (OFF_MG - OFF_Q)), F32)], axis=1)
    q["wt"] = wt.T.astype(BF16)
    q["w_pw2b"] = p["w_pw2"].astype(BF16)
    for name in ("w_nsa_o", "w_out", "w_gate", "w_up", "w_down"):
        q[name + "b"] = p[name].astype(BF16)
    return q


def _rows_from_channel_major(a):
    B, _, T = a.shape
    return jnp.transpose(a.reshape(B, N_KV, HEAD_DIM, T), (0, 3, 1, 2))


def _channel_major(a):
    N, R = a.shape[:2]
    return jnp.transpose(a, (0, 2, 3, 1)).reshape(N, KV_W, R)


def _prompt_layer(x, mod, prm, tm, tq):
    B, T, _ = x.shape
    mod = mod[:, None, :]
    f = _front(x, mod, jnp.arange(T, dtype=jnp.int32), prm, tm)
    u_out, conv_state = _conv(f["u"], jnp.zeros((B, CONV_K - 1, D_CONV), F32), prm, tm)
    kcT, vcT = _compress(f["kcmpT"], f["vcmpT"], prm, 2048)
    o = _nsa_prompt(f, jnp.swapaxes(kcT, 1, 2).astype(BF16), vcT.astype(BF16), tq)
    y = _back(x, o, u_out, f["gmg"], mod, prm, tm)
    keep = min(WINDOW, T)
    rows = _rows_from_channel_major
    return y, (rows(f["kcmpT"]), rows(f["vcmpT"]), rows(f["kselT"]), rows(f["vselT"]),
               rows(f["kwinT"][:, :, T - keep:]), rows(f["vwinT"][:, :, T - keep:]), conv_state)


def _sample_layer(x, mod, cache_cmp_k, cache_cmp_v, cache_sel_k, cache_sel_v, cache_win_k, cache_win_v,
                  state_conv, page_table, prm):
    Bs, Ts, _ = x.shape
    n_pages = page_table.shape[1]
    past_len = n_pages * PAGE
    n_tok = Bs * Ts
    tqs = 32
    assert n_tok % LANES == 0 and Ts <= tqs and n_pages % min(16, n_pages) == 0
    pos = past_len + jnp.arange(n_tok, dtype=jnp.int32) % Ts
    mod_tok = jnp.repeat(mod, Ts, axis=0)[None]
    f = _front(x.reshape(1, n_tok, D_MODEL), mod_tok, pos, prm, n_tok)
    u_out, conv_state = _conv(f["u"].reshape(Bs, Ts, D_CONV), state_conv, prm, Ts)

    def lanes(a, rows, width):
        a = a.reshape(rows + (Bs, Ts))
        a = jnp.moveaxis(a, len(rows), 0)
        return jnp.pad(a, ((0, 0),) * (len(rows) + 1) + ((0, width - Ts),))

    kcT_past, vcT_past = _compress_paged(_channel_major(cache_cmp_k), _channel_major(cache_cmp_v), page_table, prm,
                                         min(16, n_pages))
    kcT_new, vcT_new = _compress(lanes(f["kcmpT"][0], (KV_W,), BLOCK), lanes(f["vcmpT"][0], (KV_W,), BLOCK), prm, BLOCK)
    nb_past = past_len // BLOCK
    cat = lambda a, b: jnp.concatenate([a[:, :, :nb_past], b], axis=2)
    kcb = jnp.swapaxes(cat(kcT_past, kcT_new), 1, 2).astype(BF16)
    vcTb = cat(vcT_past, vcT_new).astype(BF16)

    def q_operand(qT):
        q = lanes(qT[0], (N_KV, GROUP, HEAD_DIM), tqs)
        q = jnp.moveaxis(q, 2, 3).reshape(Bs, N_KV, HEAD_DIM, GROUP * tqs)
        z = jnp.zeros_like(q[:, 0])
        return jnp.stack([jnp.concatenate([q[:, 0], z], axis=1), jnp.concatenate([z, q[:, 1]], axis=1)], axis=1)

    g = lanes(f["ngT"][0, :3 * N_HEADS], (N_KV, GROUP, 3), tqs)
    g = jnp.moveaxis(g, 2, 3).reshape(Bs, N_KV, 3, GROUP * tqs)
    gates = jnp.pad(g, ((0, 0), (0, 0), (0, 5), (0, 0)))
    rows_nat = lambda a: jnp.pad(a[..., :KV_W].reshape(Bs, Ts, KV_W), ((0, 0), (0, tqs - Ts), (0, 0)))
    rows_T = lambda a: lanes(a[0].astype(BF16), (KV_W,), tqs)
    oT = _nsa_sample(q_operand(f["qp"]), q_operand(f["qr"]), kcb, vcTb, gates,
                     _channel_major(cache_sel_k), _channel_major(cache_sel_v), page_table,
                     rows_nat(f["kselb"]), rows_T(f["vselT"]),
                     _channel_major(cache_win_k), _channel_major(cache_win_v),
                     rows_nat(f["kwinb"]), rows_T(f["vwinT"]), tqs, min(16, n_pages))
    o = oT.reshape(Bs, N_KV, HEAD_DIM, GROUP, tqs)[..., :Ts]
    o = jnp.transpose(o, (0, 4, 1, 3, 2)).reshape(1, n_tok, Q_W).astype(BF16)
    y = _back(x.reshape(1, n_tok, D_MODEL), o, u_out.reshape(1, n_tok, D_MODEL), f["gmg"], mod_tok, prm, n_tok)
    hs = lambda a: a[0].T.reshape(Bs, Ts, N_KV, HEAD_DIM)
    buf = cache_win_k.shape[1]
    keep = min(WINDOW, buf + Ts)
    kw = jnp.concatenate([cache_win_k, hs(f["kwinT"])], axis=1)[:, -keep:]
    vw = jnp.concatenate([cache_win_v, hs(f["vwinT"])], axis=1)[:, -keep:]
    return y.reshape(Bs, Ts, D_MODEL), (hs(f["kcmpT"]), hs(f["vcmpT"]), hs(f["kselT"]), hs(f["vselT"]), kw, vw, conv_state)


def kernel(x_prompt, x_sample, cache_cmp_k, cache_cmp_v, cache_sel_k, cache_sel_v, cache_win_k, cache_win_v,
           state_conv, page_table, c_prompt, c_sample, w_ada, b_ada, mix_norm_g, w_in, q_norm_g, k_norm_g,
           w_dw, b_dw, conv_ln_g, conv_ln_b, w_pw2, cmp_mod_k, cmp_w_k, cmp_mod_v, cmp_w_v, w_nsa_o, w_out,
           ffn_norm_g, w_gate, w_up, w_down):
    depth = w_in.shape[0]
    yp, ys = x_prompt, x_sample
    st_p, st_s = [], []
    for l in range(depth):
        p = dict(w_ada=w_ada[l], b_ada=b_ada[l], mix_norm_g=mix_norm_g[l], w_in=w_in[l],
                 q_norm_g=q_norm_g[l], k_norm_g=k_norm_g[l], w_dw=w_dw[l], b_dw=b_dw[l],
                 conv_ln_g=conv_ln_g[l], conv_ln_b=conv_ln_b[l], w_pw2=w_pw2[l],
                 cmp_mod_k=cmp_mod_k[l], cmp_w_k=cmp_w_k[l], cmp_mod_v=cmp_mod_v[l],
                 cmp_w_v=cmp_w_v[l], w_nsa_o=w_nsa_o[l], w_out=w_out[l],
                 ffn_norm_g=ffn_norm_g[l], w_gate=w_gate[l], w_up=w_up[l], w_down=w_down[l])
        prm = _prep_params(p)
        nb_p = c_prompt.shape[0]
        mod = _ada(jnp.concatenate([c_prompt, c_sample], axis=0), p["w_ada"], p["b_ada"])
        yp, sp = _prompt_layer(yp, mod[:nb_p], prm, 256, 128)
        ys, ss = _sample_layer(ys, mod[nb_p:], cache_cmp_k[l], cache_cmp_v[l], cache_sel_k[l], cache_sel_v[l],
                               cache_win_k[l], cache_win_v[l], state_conv[l], page_table, prm)
        st_p.append(sp)
        st_s.append(ss)
    stk = lambda lst, j: jnp.stack([s[j] for s in lst])
    return (yp, ys) + tuple(stk(st_p, j) for j in range(7)) + tuple(stk(st_s, j) for j in range(7))
```

```python
import functools
import math

import jax
import jax.numpy as jnp
from jax import lax
from jax.experimental import pallas as pl
from jax.experimental.pallas import tpu as pltpu

F32 = jnp.float32
BF16 = jnp.bfloat16

D_MODEL = 1024
D_CONV = 512
CONV_K = 31
N_HEADS = 8
N_KV = 2
HEAD_DIM = 64
GROUP = 4
Q_W = N_HEADS * HEAD_DIM
KV_W = N_KV * HEAD_DIM
BLOCK = 64
N_SEL = 16
WINDOW = 512
ROPE_THETA = 10000.0
D_FF = 2816
EPS = 1e-6
FORCED_BONUS = 2.0 * GROUP
PAGE = 128
OFF_Q = 2 * D_CONV
OFF_KV = OFF_Q + Q_W
OFF_NG = OFF_KV + 6 * KV_W
OFF_MG = OFF_NG + 3 * N_HEADS
Q_SCALE = HEAD_DIM ** -0.5 * math.log2(math.e)
NEG = -(2.0 ** 100)
M_INIT = -(2.0 ** 99)
ONES_ROWS = 16
V_AUG = HEAD_DIM + ONES_ROWS
K_AUG = KV_W + 16
N_T_ROWS = 1312
CONV_HALO = 32
SUBLANES = 8
LANES = 128
SEL_CHUNK = 512
VMEM_LIMIT = 56 * 1024 * 1024


def _cparams(sem):
    return pltpu.CompilerParams(dimension_semantics=sem, vmem_limit_bytes=VMEM_LIMIT)


def _ada_kernel(c_ref, w_ref, b_ref, o_ref):
    c = c_ref[...]
    s = c * jax.nn.sigmoid(c)
    o_ref[...] = jnp.dot(s.astype(BF16), w_ref[...].astype(BF16), preferred_element_type=F32) + b_ref[...]


def _ada(c, w_ada, b_ada):
    n, tn = c.shape[0], 1024
    return pl.pallas_call(
        _ada_kernel,
        grid=(w_ada.shape[1] // tn,),
        in_specs=[pl.BlockSpec((n, D_MODEL), lambda j: (0, 0)),
                  pl.BlockSpec((D_MODEL, tn), lambda j: (0, j)),
                  pl.BlockSpec((1, tn), lambda j: (0, j))],
        out_specs=pl.BlockSpec((n, tn), lambda j: (0, j)),
        out_shape=jax.ShapeDtypeStruct((n, w_ada.shape[1]), F32),
        compiler_params=_cparams(("arbitrary",)),
        name="ada",
    )(c, w_ada, b_ada.reshape(1, -1))


def _front_kernel(x_ref, sh1_ref, sc1_ref, g_ref, wn_ref, wt_ref, gq_ref, gk_ref, cos_ref, sin_ref,
                  u_ref, gmg_ref, qp_ref, qr_ref, kcmpT_ref, vcmpT_ref, kselT_ref, vselT_ref, kwinT_ref, vwinT_ref,
                  kselb_ref, kwinb_ref, vselTb_ref, vwinTb_ref, ngT_ref):
    x = x_ref[0]
    ms = jnp.mean(x * x, axis=-1, keepdims=True)
    h = x * lax.rsqrt(ms + EPS) * g_ref[...]
    h = h * (1.0 + sc1_ref[0]) + sh1_ref[0]
    hb = h.astype(BF16)
    zn = jnp.dot(hb, wn_ref[...], preferred_element_type=F32)
    u_ref[0] = zn[:, :D_CONV] * jax.nn.sigmoid(zn[:, D_CONV:2 * D_CONV])
    gmg_ref[0] = jax.nn.sigmoid(zn[:, 2 * D_CONV:])
    zt = lax.dot_general(wt_ref[...], hb, (((1,), (1,)), ((), ())), preferred_element_type=F32)
    cos = cos_ref[...]
    sin = sin_ref[...]
    half = HEAD_DIM // 2

    def norm(v, g):
        m = jnp.mean(v * v, axis=0, keepdims=True)
        return v * lax.rsqrt(m + EPS) * g

    def rope(v):
        a, b = v[:half], v[half:]
        return jnp.concatenate([a * cos - b * sin, b * cos + a * sin], axis=0)

    gq = gq_ref[...]
    for hh in range(N_HEADS):
        q = norm(zt[hh * HEAD_DIM:(hh + 1) * HEAD_DIM], gq) * Q_SCALE
        qp_ref[0, hh * HEAD_DIM:(hh + 1) * HEAD_DIM, :] = q.astype(BF16)
        qr_ref[0, hh * HEAD_DIM:(hh + 1) * HEAD_DIM, :] = rope(q).astype(BF16)

    def head_pair(j, g=None, rotary=False):
        outs = []
        for hh in range(N_KV):
            lo = Q_W + j * KV_W + hh * HEAD_DIM
            v = zt[lo:lo + HEAD_DIM]
            if g is not None:
                v = norm(v, g)
            if rotary:
                v = rope(v)
            outs.append(v)
        return jnp.concatenate(outs, axis=0)

    k_sel = head_pair(2, gk_ref[1], True)
    v_sel = head_pair(3)
    k_win = head_pair(4, gk_ref[2], True)
    v_win = head_pair(5)
    kcmpT_ref[0] = head_pair(0, gk_ref[0])
    vcmpT_ref[0] = head_pair(1)
    kselT_ref[0] = k_sel
    vselT_ref[0] = v_sel
    kwinT_ref[0] = k_win
    vwinT_ref[0] = v_win
    tm = x.shape[0]
    blk = lax.shift_right_logical(pl.program_id(1) * tm + lax.broadcasted_iota(jnp.int32, (tm, K_AUG - KV_W), 0), 6)
    onehot = (blk % (SEL_CHUNK // BLOCK)) == lax.broadcasted_iota(jnp.int32, (tm, K_AUG - KV_W), 1)
    kselb_ref[0] = jnp.concatenate([k_sel.T, jnp.where(onehot, 1.0, 0.0)], axis=1).astype(BF16)
    kwinb_ref[0] = k_win.T.astype(BF16)
    ones = jnp.ones((ONES_ROWS, tm), F32)

    def v_aug(v):
        return jnp.concatenate([v[:HEAD_DIM], ones, v[HEAD_DIM:], ones], axis=0).astype(BF16)

    vselTb_ref[0] = v_aug(v_sel)
    vwinTb_ref[0] = v_aug(v_win)
    ngT_ref[0] = jax.nn.sigmoid(zt[Q_W + 6 * KV_W:])


def _front(x, mod, pos, prm, tm):
    B, T, _ = x.shape
    Tm = mod.shape[1]
    tmm = 1 if Tm == 1 else tm
    half = HEAD_DIM // 2
    inv = jnp.power(ROPE_THETA, -jnp.arange(half, dtype=F32) / half)
    ang = pos.astype(F32)[:, None] * inv[None, :]
    cosT, sinT = jnp.cos(ang).T, jnp.sin(ang).T
    gq = jnp.broadcast_to(prm["q_norm_g"][:, None], (HEAD_DIM, tm))
    gk = jnp.broadcast_to(prm["k_norm_g"][:, :, None], (3, HEAD_DIM, tm))
    nt = T // tm
    tok = lambda w: pl.BlockSpec((1, tm, w), lambda b, t: (b, t, 0))
    tokT = lambda r: pl.BlockSpec((1, r, tm), lambda b, t: (b, 0, t))
    modspec = lambda k: pl.BlockSpec((1, tmm, D_MODEL), (lambda b, t: (b, t, k)) if Tm != 1 else (lambda b, t: (b, 0, k)))
    const = lambda shape: pl.BlockSpec(shape, lambda b, t: (0,) * len(shape))
    sd = jax.ShapeDtypeStruct
    outs = pl.pallas_call(
        _front_kernel,
        grid=(B, nt),
        in_specs=[tok(D_MODEL), modspec(0), modspec(1), const((1, D_MODEL)),
                  const((D_MODEL, 3 * D_MODEL)), const((N_T_ROWS, D_MODEL)),
                  const((HEAD_DIM, tm)), const((3, HEAD_DIM, tm)),
                  pl.BlockSpec((half, tm), lambda b, t: (0, t)), pl.BlockSpec((half, tm), lambda b, t: (0, t))],
        out_specs=[tok(D_CONV), tok(2 * D_MODEL), tokT(Q_W), tokT(Q_W)] + [tokT(KV_W)] * 6
                  + [tok(K_AUG), tok(KV_W), tokT(N_KV * V_AUG), tokT(N_KV * V_AUG), tokT(32)],
        out_shape=[sd((B, T, D_CONV), F32), sd((B, T, 2 * D_MODEL), F32), sd((B, Q_W, T), BF16), sd((B, Q_W, T), BF16)]
                  + [sd((B, KV_W, T), F32)] * 6
                  + [sd((B, T, K_AUG), BF16), sd((B, T, KV_W), BF16), sd((B, N_KV * V_AUG, T), BF16),
                     sd((B, N_KV * V_AUG, T), BF16), sd((B, 32, T), F32)],
        compiler_params=_cparams(("parallel", "arbitrary")),
        name="front",
    )(x, mod, mod, prm["mix_norm_g"].reshape(1, -1), prm["wn"], prm["wt"], gq, gk, cosT, sinT)
    names = ("u", "gmg", "qp", "qr", "kcmpT", "vcmpT", "kselT", "vselT", "kwinT", "vwinT",
             "kselb", "kwinb", "vselTb", "vwinTb", "ngT")
    return dict(zip(names, outs))


def _conv_kernel(u_ref, st_ref, wdw_ref, bdw_ref, lng_ref, lnb_ref, wpw_ref, uo_ref, so_ref, ext_ref, sh_ref, *, tm):
    t = pl.program_id(1)
    pad = CONV_HALO - (CONV_K - 1)
    n_sh = tm + CONV_HALO - SUBLANES

    @pl.when(t == 0)
    def _():
        ext_ref[0:pad, :] = jnp.zeros((pad, D_CONV), F32)
        ext_ref[pad:CONV_HALO, :] = st_ref[0]

    ext_ref[CONV_HALO:CONV_HALO + tm, :] = u_ref[0]
    for r in range(1, SUBLANES):
        sh_ref[r - 1] = ext_ref[r:r + n_sh, :]
    y = jnp.zeros((tm, D_CONV), F32) + bdw_ref[...]
    for k in range(CONV_K):
        a, r = divmod(pad + k, SUBLANES)
        rows = ext_ref[SUBLANES * a:SUBLANES * a + tm, :] if r == 0 else sh_ref[r - 1, SUBLANES * a:SUBLANES * a + tm, :]
        y = y + rows * wdw_ref[k:k + 1, :]
    mu = jnp.mean(y, axis=-1, keepdims=True)
    yc = y - mu
    var = jnp.mean(yc * yc, axis=-1, keepdims=True)
    yn = yc * lax.rsqrt(var + EPS) * lng_ref[...] + lnb_ref[...]
    ya = yn * jax.nn.sigmoid(yn)
    uo_ref[0] = jnp.dot(ya.astype(BF16), wpw_ref[...], preferred_element_type=F32)
    so_ref[0] = ext_ref[tm + pad:tm + CONV_HALO, :]
    carry = ext_ref[tm:tm + CONV_HALO, :]
    ext_ref[0:CONV_HALO, :] = carry


def _conv(u, state, prm, tm):
    B, T, _ = u.shape
    const = lambda shape: pl.BlockSpec(shape, lambda b, t: (0,) * len(shape))
    return pl.pallas_call(
        functools.partial(_conv_kernel, tm=tm),
        grid=(B, T // tm),
        in_specs=[pl.BlockSpec((1, tm, D_CONV), lambda b, t: (b, t, 0)),
                  pl.BlockSpec((1, CONV_K - 1, D_CONV), lambda b, t: (b, 0, 0)),
                  const((CONV_K, D_CONV)), const((1, D_CONV)), const((1, D_CONV)), const((1, D_CONV)),
                  const((D_CONV, D_MODEL))],
        out_specs=[pl.BlockSpec((1, tm, D_MODEL), lambda b, t: (b, t, 0)),
                   pl.BlockSpec((1, CONV_K - 1, D_CONV), lambda b, t: (b, 0, 0))],
        out_shape=[jax.ShapeDtypeStruct((B, T, D_MODEL), F32), jax.ShapeDtypeStruct((B, CONV_K - 1, D_CONV), F32)],
        scratch_shapes=[pltpu.VMEM((CONV_HALO + tm, D_CONV), F32),
                        pltpu.VMEM((SUBLANES - 1, tm + CONV_HALO - SUBLANES, D_CONV), F32)],
        compiler_params=_cparams(("parallel", "arbitrary")),
        name="conv",
    )(u, state, prm["w_dw"], prm["b_dw"].reshape(1, -1), prm["conv_ln_g"].reshape(1, -1),
      prm["conv_ln_b"].reshape(1, -1), prm["w_pw2b"])


def _gather_pages(pt_ref, srcs, buf_ref, sem_ref, pages_per_step):
    step = pl.program_id(0) * pl.num_programs(1) + pl.program_id(1)
    n_steps = pl.num_programs(0) * pl.num_programs(1)
    slot = step % 2

    def copies(s, slot_):
        return [pltpu.make_async_copy(src.at[pt_ref[s * pages_per_step + i]], buf_ref.at[slot_, a, i], sem_ref.at[slot_])
                for a, src in enumerate(srcs) for i in range(pages_per_step)]

    @pl.when(step == 0)
    def _():
        for c in copies(step, slot):
            c.start()

    @pl.when(step + 1 < n_steps)
    def _():
        for c in copies(step + 1, 1 - slot):
            c.start()

    for c in copies(step, slot):
        c.wait()
    return [jnp.concatenate([buf_ref[slot, a, i] for i in range(pages_per_step)], axis=1) for a in range(len(srcs))]


def _compress_core(xk, xv, modk_ref, wk_ref, modv_ref, wv_ref, s_ref, kc_ref, vc_ref, acc_ref, group):
    slot = pl.program_id(1) % group

    @pl.when(slot == 0)
    def _():
        acc_ref[...] = jnp.zeros_like(acc_ref)

    place = s_ref[slot]
    rows = [(x * (1.0 + mod_ref[...])).astype(BF16) for x, mod_ref in ((xk, modk_ref), (xv, modv_ref))]
    sums = jnp.dot(jnp.concatenate(rows, axis=0), place, preferred_element_type=F32)
    for i, (w_ref, out_ref) in enumerate(((wk_ref, kc_ref), (wv_ref, vc_ref))):
        acc_ref[i] += sums[i * KV_W:(i + 1) * KV_W]
        out_ref[0] = jnp.dot(w_ref[...], acc_ref[i].astype(BF16), preferred_element_type=F32)


def _compress_kernel(k_ref, v_ref, *refs, group):
    _compress_core(k_ref[0], v_ref[0], *refs, group=group)


def _compress_paged_kernel(pt_ref, k_hbm, v_hbm, *refs, group, pages_per_step):
    *core, buf_ref, sem_ref = refs
    xk, xv = _gather_pages(pt_ref, (k_hbm, v_hbm), buf_ref, sem_ref, pages_per_step)
    _compress_core(xk, xv, *core, group=group)


def _compress_consts(prm, cols):
    nbs = max(cols // BLOCK, 1)
    group = LANES // nbs
    modT = lambda m: jnp.tile(m.T, (N_KV, max(cols // BLOCK, 1)))[:, :cols]
    wT = lambda w: jnp.kron(jnp.eye(N_KV, dtype=F32), w.T).astype(BF16)
    r = jnp.arange(cols)[None, :, None]
    g = jnp.arange(group)[:, None, None]
    lane = jnp.arange(LANES)[None, None, :]
    place = jnp.where(lane == g * nbs + r // BLOCK, 1.0 / BLOCK, 0.0).astype(BF16)
    return (modT(prm["cmp_mod_k"]), wT(prm["cmp_w_k"]), modT(prm["cmp_mod_v"]), wT(prm["cmp_w_v"]), place), group


def _compress_specs(cols, group, n_batch, n_out_blocks):
    const = lambda shape: pl.BlockSpec(shape, lambda *a: (0,) * len(shape))
    ospec = pl.BlockSpec((1, KV_W, LANES), lambda b, j, *a: (b, 0, j // group))
    osh = jax.ShapeDtypeStruct((n_batch, KV_W, n_out_blocks * LANES), F32)
    const_specs = [const((KV_W, cols)), const((KV_W, KV_W)), const((KV_W, cols)), const((KV_W, KV_W)),
                   const((group, cols, LANES))]
    return const_specs, [ospec, ospec], [osh, osh], [pltpu.VMEM((2, KV_W, LANES), F32)]


def _compress(kT, vT, prm, cols):
    B, _, L = kT.shape
    cols = min(cols, L)
    n_blocks = -(-(L // BLOCK) // LANES) if L >= BLOCK else 1
    consts, group = _compress_consts(prm, cols)
    const_specs, out_specs, out_shape, scratch = _compress_specs(cols, group, B, n_blocks)
    x_spec = pl.BlockSpec((1, KV_W, cols), lambda b, j: (b, 0, j))
    return pl.pallas_call(
        functools.partial(_compress_kernel, group=group), grid=(B, L // cols),
        in_specs=[x_spec, x_spec] + const_specs, out_specs=out_specs, out_shape=out_shape, scratch_shapes=scratch,
        compiler_params=_cparams(("parallel", "arbitrary")), name="compress",
    )(kT, vT, *consts)


def _compress_paged(cache_kT, cache_vT, page_table, prm, pages_per_step):
    B, n_pages = page_table.shape
    cols = pages_per_step * PAGE
    n_blocks = -(-(n_pages * PAGE // BLOCK) // LANES)
    consts, group = _compress_consts(prm, cols)
    const_specs, out_specs, out_shape, scratch = _compress_specs(cols, group, B, n_blocks)
    hbm = pl.BlockSpec(memory_space=pl.ANY)
    grid_spec = pltpu.PrefetchScalarGridSpec(
        num_scalar_prefetch=1, grid=(B, n_pages // pages_per_step),
        in_specs=[hbm, hbm] + const_specs, out_specs=out_specs,
        scratch_shapes=scratch + [pltpu.VMEM((2, 2, pages_per_step, KV_W, PAGE), F32), pltpu.SemaphoreType.DMA((2,))])
    return pl.pallas_call(
        functools.partial(_compress_paged_kernel, group=group, pages_per_step=pages_per_step),
        grid_spec=grid_spec, out_shape=out_shape,
        compiler_params=_cparams(("arbitrary", "arbitrary")), name="compress_paged",
    )(page_table.reshape(-1), cache_kT, cache_vT, *consts)


def _attend_scores(s, vT, mask, carry, s_max=None):
    m, acc = carry
    if mask is not None:
        s = jnp.where(mask, s, NEG)
    if s_max is None:
        s_max = jnp.max(s, axis=0, keepdims=True)
    m_new = jnp.maximum(m, s_max)
    alpha = jnp.exp2(m - m_new)
    p = jnp.exp2(s - m_new).astype(BF16)
    return m_new, alpha * acc + jnp.dot(vT, p, preferred_element_type=F32)


def _attend(qT, k, vT, mask, carry):
    return _attend_scores(jnp.dot(k, qT, preferred_element_type=F32), vT, mask, carry)


def _attend_init(L):
    return jnp.full((1, L), M_INIT, F32), jnp.zeros((V_AUG, L), F32)


def _attend_out(carry):
    _, acc = carry
    return acc[:HEAD_DIM] / jnp.maximum(acc[HEAD_DIM:HEAD_DIM + 1], 1e-30)


def _with_ones(vT):
    return jnp.concatenate([vT, jnp.ones((ONES_ROWS, vT.shape[1]), BF16)], axis=0)


def _block_mask(sel_rows, kc):
    nb, L = sel_rows.shape
    return jnp.broadcast_to((sel_rows > 0.5)[:, None, :], (nb, BLOCK, L)).reshape(kc, L)


def _cmp_and_select(qpT, kc, vcT, q0, tq):
    return _cmp_finish(jnp.dot(kc, qpT, preferred_element_type=F32), vcT, q0, tq)


def _cmp_finish(s, vcT, q0, tq):
    nb, L = s.shape
    blk = lax.broadcasted_iota(jnp.int32, (nb, L), 0)
    qpos = q0 + lax.broadcasted_iota(jnp.int32, (1, L), 1) % tq
    ok = (blk + 1) * BLOCK <= qpos + 1
    s = jnp.where(ok, s, -jnp.inf)
    m = jnp.max(s, axis=0, keepdims=True)
    m = jnp.where(m > -jnp.inf, m, 0.0)
    e = jnp.where(ok, jnp.exp2(s - m), 0.0)
    p = e / jnp.maximum(jnp.sum(e, axis=0, keepdims=True), 1e-30)
    o_cmp = jnp.dot(vcT, p.astype(BF16), preferred_element_type=F32)
    if tq % LANES == 0:
        w = tq
        imp = p[:, :tq]
        for g in range(1, GROUP):
            imp = imp + p[:, g * tq:(g + 1) * tq]
    else:
        w = L
        imp = p
        for g in range(1, GROUP):
            imp = imp + pltpu.roll(p, g * tq, 1)
    if w != L:
        blk = lax.broadcasted_iota(jnp.int32, (nb, w), 0)
        qpos = q0 + lax.broadcasted_iota(jnp.int32, (1, w), 1)
    cur = lax.shift_right_logical(qpos, 6)
    visible = blk <= cur
    forced = (blk == 0) | (blk == cur) | (blk == cur - 1)
    score = jnp.where(visible, imp + jnp.where(forced, FORCED_BONUS, 0.0), -1.0)
    ridx = blk.astype(F32)
    sel = jnp.zeros_like(score)
    rem = score
    for _ in range(min(N_SEL, nb)):
        mx = jnp.max(rem, axis=0, keepdims=True)
        first = jnp.min(jnp.where(rem == mx, ridx, float(nb)), axis=0, keepdims=True)
        pick = ridx == first
        rem = jnp.where(pick, -jnp.inf, rem)
        sel = jnp.where(pick, 1.0, sel)
    if w != L:
        sel = jnp.concatenate([sel] * GROUP, axis=1)
    return o_cmp, sel


def _nsa_prompt_kernel(qp_ref, qr_ref, kc_ref, vcT_ref, ksel_ref, vselT_ref, kwin_ref, vwinT_ref, ngT_ref,
                       o_ref, bias_ref, s_ref, smax_ref, sel_ref, win_ref, diag_ref, *, tq):
    t = pl.program_id(1)
    s0 = t * tq
    L = GROUP * tq
    lane = lax.broadcasted_iota(jnp.int32, (1, L), 1)
    qpos = s0 + lane % tq
    ng = ngT_ref[0]
    heads = [slice(h * HEAD_DIM, (h + 1) * HEAD_DIM) for h in range(N_KV)]

    def qpad(ref, h):
        q = jnp.concatenate([ref[0, (h * GROUP + g) * HEAD_DIM:(h * GROUP + g + 1) * HEAD_DIM, :]
                             for g in range(GROUP)], axis=1)
        z = jnp.zeros_like(q)
        return jnp.concatenate([q, z] if h == 0 else [z, q], axis=0)

    qr = [qpad(qr_ref, h) for h in range(N_KV)]
    vrows = [slice(h * V_AUG, (h + 1) * V_AUG) for h in range(N_KV)]
    n_blk = SEL_CHUNK // BLOCK
    n_full = s0 // SEL_CHUNK
    k_diag = pl.multiple_of(n_full * SEL_CHUNK, SEL_CHUNK)
    w0 = pl.multiple_of(jnp.maximum(s0 - WINDOW, 0), tq)

    s_cmp = [jnp.dot(kc_ref[0], qpad(qp_ref, h), preferred_element_type=F32) for h in range(N_KV)]
    for h in range(N_KV):
        win_ref[h] = jnp.dot(kwin_ref[0, pl.ds(w0, WINDOW + tq), :], qr[h], preferred_element_type=F32)
        diag_ref[h] = jnp.dot(ksel_ref[0, pl.ds(k_diag, SEL_CHUNK), pl.ds(0, KV_W)], qr[h], preferred_element_type=F32)

    o_cmp = []
    for h in range(N_KV):
        o, sel = _cmp_finish(s_cmp[h], vcT_ref[0, heads[h], :], s0, tq)
        o_cmp.append(o)
        sel_ref[h] = sel
        bias = jnp.where(sel > 0.5, 0.0, NEG).reshape(-1, n_blk, L)
        bias = jnp.concatenate([bias, jnp.full((1, n_blk, L), NEG, F32)], axis=0)
        bias_ref[h] = jnp.concatenate([bias, jnp.zeros_like(bias)], axis=1).astype(BF16)

    dead = bias_ref.shape[1] - 1

    def scores(c, bias_idx, h):
        k0 = pl.multiple_of(c * SEL_CHUNK, SEL_CHUNK)
        return jnp.dot(ksel_ref[0, pl.ds(k0, SEL_CHUNK), :], jnp.concatenate([qr[h], bias_ref[h, bias_idx]], axis=0),
                       preferred_element_type=F32)

    def issue(c, slot):
        c_eff = jnp.minimum(c, jnp.maximum(n_full - 1, 0))
        for h in range(N_KV):
            s = scores(c_eff, jnp.where(c < n_full, c, dead), h)
            s_ref[slot, h] = s
            smax_ref[slot, h] = jnp.max(s, axis=0, keepdims=True)

    def consume(c, slot, carries):
        k0 = pl.multiple_of(jnp.minimum(c, jnp.maximum(n_full - 1, 0)) * SEL_CHUNK, SEL_CHUNK)
        return tuple(_attend_scores(s_ref[slot, h], vselT_ref[0, vrows[h], pl.ds(k0, SEL_CHUNK)], None, carries[h],
                                    smax_ref[slot, h]) for h in range(N_KV))

    issue(0, 0)

    kpos = w0 + lax.broadcasted_iota(jnp.int32, (WINDOW + tq, L), 0)
    wmask = (kpos <= qpos) & (kpos > qpos - WINDOW)
    o_win = [_attend_out(_attend_scores(win_ref[h], vwinT_ref[0, vrows[h], pl.ds(w0, WINDOW + tq)], wmask, _attend_init(L)))
             for h in range(N_KV)]

    causal = k_diag + lax.broadcasted_iota(jnp.int32, (SEL_CHUNK, L), 0) <= qpos
    r_diag = pl.multiple_of(n_full * n_blk, n_blk)
    carries = tuple(_attend_scores(diag_ref[h], vselT_ref[0, vrows[h], pl.ds(k_diag, SEL_CHUNK)],
                                   _block_mask(sel_ref[h, pl.ds(r_diag, n_blk), :], SEL_CHUNK) & causal, _attend_init(L))
                    for h in range(N_KV))

    def pair(i, carries):
        c = 2 * i
        issue(c + 1, 1)
        carries = consume(c, 0, carries)
        issue(c + 2, 0)
        return consume(c + 1, 1, carries)

    carries = lax.fori_loop(0, (n_full + 1) // 2, pair, carries)

    def gate(h, br):
        return jnp.concatenate([ng[h * 12 + g * 3 + br:h * 12 + g * 3 + br + 1, :] for g in range(GROUP)], axis=1)

    for h in range(N_KV):
        o_sel = _attend_out(carries[h])
        o = gate(h, 0) * o_cmp[h] + gate(h, 1) * o_sel + gate(h, 2) * o_win[h]
        oT = jnp.concatenate([o[:, g * tq:(g + 1) * tq] for g in range(GROUP)], axis=0)
        o_ref[0, :, h * GROUP * HEAD_DIM:(h + 1) * GROUP * HEAD_DIM] = oT.T.astype(o_ref.dtype)


def _nsa_prompt(f, kcb, vcTb, tq):
    B, _, T = f["qp"].shape
    assert T % SEL_CHUNK == 0 and SEL_CHUNK % tq == 0 and WINDOW % tq == 0 and T >= WINDOW + tq
    nbp = kcb.shape[1]
    full = lambda shape: pl.BlockSpec((1,) + shape, lambda b, t: (b, 0, 0))
    return pl.pallas_call(
        functools.partial(_nsa_prompt_kernel, tq=tq),
        grid=(B, T // tq),
        in_specs=[pl.BlockSpec((1, Q_W, tq), lambda b, t: (b, 0, t)), pl.BlockSpec((1, Q_W, tq), lambda b, t: (b, 0, t)),
                  full((nbp, KV_W)), full((KV_W, nbp)),
                  full((T, K_AUG)), full((N_KV * V_AUG, T)), full((T, KV_W)), full((N_KV * V_AUG, T)),
                  pl.BlockSpec((1, 32, tq), lambda b, t: (b, 0, t))],
        out_specs=pl.BlockSpec((1, tq, Q_W), lambda b, t: (b, t, 0)),
        out_shape=jax.ShapeDtypeStruct((B, T, Q_W), BF16),
        scratch_shapes=[pltpu.VMEM((N_KV, nbp // (SEL_CHUNK // BLOCK) + 1, 2 * (SEL_CHUNK // BLOCK), GROUP * tq), BF16),
                        pltpu.VMEM((2, N_KV, SEL_CHUNK, GROUP * tq), F32), pltpu.VMEM((2, N_KV, 1, GROUP * tq), F32),
                        pltpu.VMEM((N_KV, nbp, GROUP * tq), F32),
                        pltpu.VMEM((N_KV, WINDOW + tq, GROUP * tq), F32),
                        pltpu.VMEM((N_KV, SEL_CHUNK, GROUP * tq), F32)],
        compiler_params=_cparams(("parallel", "arbitrary")),
        name="nsa_prompt",
    )(f["qp"], f["qr"], kcb, vcTb, f["kselb"], f["vselTb"], f["kwinb"], f["vwinTb"], f["ngT"])


def _nsa_sample_kernel(pt_ref, qp_ref, qr_ref, kc_ref, vcT_ref, gate_ref, k_hbm, v_hbm,
                       knew_ref, vnewT_ref, cwkT_ref, cwvT_ref, kwnew_ref, vwnewT_ref,
                       o_ref, sel_ref, ocmp_ref, m_ref, acc_ref, buf_ref, sem_ref, *, tqs, past_len, pages_per_step):
    kT_pages, vT_pages = _gather_pages(pt_ref, (k_hbm, v_hbm), buf_ref, sem_ref, pages_per_step)
    j = pl.program_id(1)
    n_steps = pl.num_programs(1)
    L = GROUP * tqs
    kc_step = pages_per_step * PAGE
    lane = lax.broadcasted_iota(jnp.int32, (1, L), 1)
    qi = lane % tqs
    qpos = past_len + qi
    heads = [slice(h * HEAD_DIM, (h + 1) * HEAD_DIM) for h in range(N_KV)]

    @pl.when(j == 0)
    def _():
        for h in range(N_KV):
            o_cmp, sel = _cmp_and_select(qp_ref[0, h], kc_ref[0], vcT_ref[0, heads[h], :], past_len, tqs)
            sel_ref[h] = sel
            ocmp_ref[h] = o_cmp
            m_ref[h], acc_ref[h] = _attend_init(L)

    k = kT_pages.T.astype(BF16)
    vT = vT_pages.astype(BF16)
    r0 = pl.multiple_of(j * (kc_step // BLOCK), kc_step // BLOCK)
    scores = [jnp.dot(k, qr_ref[0, h], preferred_element_type=F32) for h in range(N_KV)]
    for h in range(N_KV):
        mask = _block_mask(sel_ref[h, pl.ds(r0, kc_step // BLOCK), :], kc_step)
        m_ref[h], acc_ref[h] = _attend_scores(scores[h], _with_ones(vT[heads[h]]), mask, (m_ref[h], acc_ref[h]))

    @pl.when(j == n_steps - 1)
    def _():
        nb_past = past_len // BLOCK
        buf = cwkT_ref.shape[2]
        r_new = lax.broadcasted_iota(jnp.int32, (tqs, L), 0)
        r_buf = lax.broadcasted_iota(jnp.int32, (buf, L), 0)
        cwk = cwkT_ref[0].T.astype(BF16)
        cwvT = cwvT_ref[0].astype(BF16)
        gates = gate_ref[0]
        for h in range(N_KV):
            qr = qr_ref[0, h]
            mask = (sel_ref[h, nb_past:nb_past + 1, :] > 0.5) & (r_new <= qi)
            carry = _attend(qr, knew_ref[0], _with_ones(vnewT_ref[0, heads[h], :]), mask, (m_ref[h], acc_ref[h]))
            o_sel = _attend_out(carry)
            carry = _attend(qr, cwk, _with_ones(cwvT[heads[h]]), r_buf > qi + (buf - WINDOW), _attend_init(L))
            carry = _attend(qr, kwnew_ref[0], _with_ones(vwnewT_ref[0, heads[h], :]), r_new <= qi, carry)
            o_win = _attend_out(carry)
            g = gates[h]
            o_ref[0, h] = g[0:1] * ocmp_ref[h] + g[1:2] * o_sel + g[2:3] * o_win


def _nsa_sample(qpT, qrT, kcb, vcTb, gates, cache_sel_kT, cache_sel_vT, page_table, knew, vnewT,
                cache_win_kT, cache_win_vT, kwnew, vwnewT, tqs, pages_per_step):
    Bs, n_pages = page_table.shape
    L = GROUP * tqs
    nbp = kcb.shape[1]
    buf = cache_win_kT.shape[2]
    per_b = lambda shape: pl.BlockSpec((1,) + shape, lambda b, j, pt: (b,) + (0,) * len(shape))
    hbm = pl.BlockSpec(memory_space=pl.ANY)
    grid_spec = pltpu.PrefetchScalarGridSpec(
        num_scalar_prefetch=1,
        grid=(Bs, n_pages // pages_per_step),
        in_specs=[per_b((N_KV, KV_W, L)), per_b((N_KV, KV_W, L)), per_b((nbp, KV_W)), per_b((KV_W, nbp)),
                  per_b((N_KV, 8, L))]
                 + [hbm, hbm]
                 + [per_b((tqs, KV_W)), per_b((KV_W, tqs)), per_b((KV_W, buf)), per_b((KV_W, buf)),
                    per_b((tqs, KV_W)), per_b((KV_W, tqs))],
        out_specs=per_b((N_KV, HEAD_DIM, L)),
        scratch_shapes=[pltpu.VMEM((N_KV, nbp, L), F32), pltpu.VMEM((N_KV, HEAD_DIM, L), F32),
                        pltpu.VMEM((N_KV, 1, L), F32), pltpu.VMEM((N_KV, V_AUG, L), F32),
                        pltpu.VMEM((2, 2, pages_per_step, KV_W, PAGE), F32), pltpu.SemaphoreType.DMA((2,))],
    )
    return pl.pallas_call(
        functools.partial(_nsa_sample_kernel, tqs=tqs, past_len=n_pages * PAGE, pages_per_step=pages_per_step),
        grid_spec=grid_spec,
        out_shape=jax.ShapeDtypeStruct((Bs, N_KV, HEAD_DIM, L), F32),
        compiler_params=_cparams(("arbitrary", "arbitrary")),
        name="nsa_sample",
    )(page_table.reshape(-1), qpT, qrT, kcb, vcTb, gates, cache_sel_kT, cache_sel_vT,
      knew, vnewT, cache_win_kT, cache_win_vT, kwnew, vwnewT)


def _back_kernel(x_ref, o_ref, uo_ref, gmg_ref, g1_ref, sh2_ref, sc2_ref, g2_ref, fg_ref,
                 wno_ref, wout_ref, wg_ref, wu_ref, wd_ref, y_ref):
    nsa = jnp.dot(o_ref[0], wno_ref[...], preferred_element_type=F32)
    gmg = gmg_ref[0]
    merged = gmg[:, :D_MODEL] * uo_ref[0] + gmg[:, D_MODEL:] * nsa
    mix = jnp.dot(merged.astype(BF16), wout_ref[...], preferred_element_type=F32)
    x1 = x_ref[0] + g1_ref[0] * mix
    ms = jnp.mean(x1 * x1, axis=-1, keepdims=True)
    h = x1 * lax.rsqrt(ms + EPS) * fg_ref[...]
    h = h * (1.0 + sc2_ref[0]) + sh2_ref[0]
    hb = h.astype(BF16)
    a = jnp.dot(hb, wg_ref[...], preferred_element_type=F32)
    b = jnp.dot(hb, wu_ref[...], preferred_element_type=F32)
    act = a * jax.nn.sigmoid(a) * b
    ffn = jnp.dot(act.astype(BF16), wd_ref[...], preferred_element_type=F32)
    y_ref[0] = x1 + g2_ref[0] * ffn


def _back(x, o, u_out, gmg, mod, prm, tm):
    B, T, _ = x.shape
    Tm = mod.shape[1]
    tmm = 1 if Tm == 1 else tm
    tok = lambda w: pl.BlockSpec((1, tm, w), lambda b, t: (b, t, 0))
    modspec = lambda k: pl.BlockSpec((1, tmm, D_MODEL), (lambda b, t: (b, t, k)) if Tm != 1 else (lambda b, t: (b, 0, k)))
    wspec = lambda shape: pl.BlockSpec(shape, lambda b, t: (0, 0), pipeline_mode=pl.Buffered(1))
    return pl.pallas_call(
        _back_kernel,
        grid=(B, T // tm),
        in_specs=[tok(D_MODEL), tok(Q_W), tok(D_MODEL), tok(2 * D_MODEL),
                  modspec(2), modspec(3), modspec(4), modspec(5), wspec((1, D_MODEL)),
                  wspec((Q_W, D_MODEL)), wspec((D_MODEL, D_MODEL)), wspec((D_MODEL, D_FF)), wspec((D_MODEL, D_FF)),
                  wspec((D_FF, D_MODEL))],
        out_specs=tok(D_MODEL),
        out_shape=jax.ShapeDtypeStruct((B, T, D_MODEL), F32),
        compiler_params=_cparams(("parallel", "arbitrary")),
        name="back",
    )(x, o, u_out, gmg, mod, mod, mod, mod, prm["ffn_norm_g"].reshape(1, -1),
      prm["w_nsa_ob"], prm["w_outb"], prm["w_gateb"], prm["w_upb"], prm["w_downb"])


def _prep_params(p):
    w_in = p["w_in"]
    q = dict(p)
    q["wn"] = jnp.concatenate([w_in[:, :OFF_Q], w_in[:, OFF_MG:]], axis=1).astype(BF16)
    wt = jnp.concatenate([w_in[:, OFF_Q:OFF_MG], jnp.zeros((D_MODEL, N_T_ROWS - (OFF_MG - OFF_Q)), F32)], axis=1)
    q["wt"] = wt.T.astype(BF16)
    q["w_pw2b"] = p["w_pw2"].astype(BF16)
    for name in ("w_nsa_o", "w_out", "w_gate", "w_up", "w_down"):
        q[name + "b"] = p[name].astype(BF16)
    return q


def _rows_from_channel_major(a):
    B, _, T = a.shape
    return jnp.transpose(a.reshape(B, N_KV, HEAD_DIM, T), (0, 3, 1, 2))


def _channel_major(a):
    N, R = a.shape[:2]
    return jnp.transpose(a, (0, 2, 3, 1)).reshape(N, KV_W, R)


def _prompt_layer(x, mod, prm, tm, tq):
    B, T, _ = x.shape
    mod = mod[:, None, :]
    f = _front(x, mod, jnp.arange(T, dtype=jnp.int32), prm, tm)
    u_out, conv_state = _conv(f["u"], jnp.zeros((B, CONV_K - 1, D_CONV), F32), prm, tm)
    kcT, vcT = _compress(f["kcmpT"], f["vcmpT"], prm, 2048)
    o = _nsa_prompt(f, jnp.swapaxes(kcT, 1, 2).astype(BF16), vcT.astype(BF16), tq)
    y = _back(x, o, u_out, f["gmg"], mod, prm, tm)
    keep = min(WINDOW, T)
    rows = _rows_from_channel_major
    return y, (rows(f["kcmpT"]), rows(f["vcmpT"]), rows(f["kselT"]), rows(f["vselT"]),
               rows(f["kwinT"][:, :, T - keep:]), rows(f["vwinT"][:, :, T - keep:]), conv_state)


def _sample_layer(x, mod, cache_cmp_k, cache_cmp_v, cache_sel_k, cache_sel_v, cache_win_k, cache_win_v,
                  state_conv, page_table, prm):
    Bs, Ts, _ = x.shape
    n_pages = page_table.shape[1]
    past_len = n_pages * PAGE
    n_tok = Bs * Ts
    tqs = 32
    assert n_tok % LANES == 0 and Ts <= tqs and n_pages % min(16, n_pages) == 0
    pos = past_len + jnp.arange(n_tok, dtype=jnp.int32) % Ts
    mod_tok = jnp.repeat(mod, Ts, axis=0)[None]
    f = _front(x.reshape(1, n_tok, D_MODEL), mod_tok, pos, prm, n_tok)
    u_out, conv_state = _conv(f["u"].reshape(Bs, Ts, D_CONV), state_conv, prm, Ts)

    def lanes(a, rows, width):
        a = a.reshape(rows + (Bs, Ts))
        a = jnp.moveaxis(a, len(rows), 0)
        return jnp.pad(a, ((0, 0),) * (len(rows) + 1) + ((0, width - Ts),))

    kcT_past, vcT_past = _compress_paged(_channel_major(cache_cmp_k), _channel_major(cache_cmp_v), page_table, prm,
                                         min(16, n_pages))
    kcT_new, vcT_new = _compress(lanes(f["kcmpT"][0], (KV_W,), BLOCK), lanes(f["vcmpT"][0], (KV_W,), BLOCK), prm, BLOCK)
    nb_past = past_len // BLOCK
    cat = lambda a, b: jnp.concatenate([a[:, :, :nb_past], b], axis=2)
    kcb = jnp.swapaxes(cat(kcT_past, kcT_new), 1, 2).astype(BF16)
    vcTb = cat(vcT_past, vcT_new).astype(BF16)

    def q_operand(qT):
        q = lanes(qT[0], (N_KV, GROUP, HEAD_DIM), tqs)
        q = jnp.moveaxis(q, 2, 3).reshape(Bs, N_KV, HEAD_DIM, GROUP * tqs)
        z = jnp.zeros_like(q[:, 0])
        return jnp.stack([jnp.concatenate([q[:, 0], z], axis=1), jnp.concatenate([z, q[:, 1]], axis=1)], axis=1)

    g = lanes(f["ngT"][0, :3 * N_HEADS], (N_KV, GROUP, 3), tqs)
    g = jnp.moveaxis(g, 2, 3).reshape(Bs, N_KV, 3, GROUP * tqs)
    gates = jnp.pad(g, ((0, 0), (0, 0), (0, 5), (0, 0)))
    rows_nat = lambda a: jnp.pad(a[..., :KV_W].reshape(Bs, Ts, KV_W), ((0, 0), (0, tqs - Ts), (0, 0)))
    rows_T = lambda a: lanes(a[0].astype(BF16), (KV_W,), tqs)
    oT = _nsa_sample(q_operand(f["qp"]), q_operand(f["qr"]), kcb, vcTb, gates,
                     _channel_major(cache_sel_k), _channel_major(cache_sel_v), page_table,
                     rows_nat(f["kselb"]), rows_T(f["vselT"]),
                     _channel_major(cache_win_k), _channel_major(cache_win_v),
                     rows_nat(f["kwinb"]), rows_T(f["vwinT"]), tqs, min(16, n_pages))
    o = oT.reshape(Bs, N_KV, HEAD_DIM, GROUP, tqs)[..., :Ts]
    o = jnp.transpose(o, (0, 4, 1, 3, 2)).reshape(1, n_tok, Q_W).astype(BF16)
    y = _back(x.reshape(1, n_tok, D_MODEL), o, u_out.reshape(1, n_tok, D_MODEL), f["gmg"], mod_tok, prm, n_tok)
    hs = lambda a: a[0].T.reshape(Bs, Ts, N_KV, HEAD_DIM)
    buf = cache_win_k.shape[1]
    keep = min(WINDOW, buf + Ts)
    kw = jnp.concatenate([cache_win_k, hs(f["kwinT"])], axis=1)[:, -keep:]
    vw = jnp.concatenate([cache_win_v, hs(f["vwinT"])], axis=1)[:, -keep:]
    return y.reshape(Bs, Ts, D_MODEL), (hs(f["kcmpT"]), hs(f["vcmpT"]), hs(f["kselT"]), hs(f["vselT"]), kw, vw, conv_state)


def kernel(x_prompt, x_sample, cache_cmp_k, cache_cmp_v, cache_sel_k, cache_sel_v, cache_win_k, cache_win_v,
           state_conv, page_table, c_prompt, c_sample, w_ada, b_ada, mix_norm_g, w_in, q_norm_g, k_norm_g,
           w_dw, b_dw, conv_ln_g, conv_ln_b, w_pw2, cmp_mod_k, cmp_w_k, cmp_mod_v, cmp_w_v, w_nsa_o, w_out,
           ffn_norm_g, w_gate, w_up, w_down):
    depth = w_in.shape[0]
    yp, ys = x_prompt, x_sample
    st_p, st_s = [], []
    for l in range(depth):
        p = dict(w_ada=w_ada[l], b_ada=b_ada[l], mix_norm_g=mix_norm_g[l], w_in=w_in[l],
                 q_norm_g=q_norm_g[l], k_norm_g=k_norm_g[l], w_dw=w_dw[l], b_dw=b_dw[l],
                 conv_ln_g=conv_ln_g[l], conv_ln_b=conv_ln_b[l], w_pw2=w_pw2[l],
                 cmp_mod_k=cmp_mod_k[l], cmp_w_k=cmp_w_k[l], cmp_mod_v=cmp_mod_v[l],
                 cmp_w_v=cmp_w_v[l], w_nsa_o=w_nsa_o[l], w_out=w_out[l],
                 ffn_norm_g=ffn_norm_g[l], w_gate=w_gate[l], w_up=w_up[l], w_down=w_down[l])
        prm = _prep_params(p)
        nb_p = c_prompt.shape[0]
        mod = _ada(jnp.concatenate([c_prompt, c_sample], axis=0), p["w_ada"], p["b_ada"])
        yp, sp = _prompt_layer(yp, mod[:nb_p], prm, 256, 128)
        ys, ss = _sample_layer(ys, mod[nb_p:], cache_cmp_k[l], cache_cmp_v[l], cache_sel_k[l], cache_sel_v[l],
                               cache_win_k[l], cache_win_v[l], state_conv[l], page_table, prm)
        st_p.append(sp)
        st_s.append(ss)
    stk = lambda lst, j: jnp.stack([s[j] for s in lst])
    return (yp, ys) + tuple(stk(st_p, j) for j in range(7)) + tuple(stk(st_s, j) for j in range(7))
```

```python
import functools
import math

import jax
import jax.numpy as jnp
from jax import lax
from jax.experimental import pallas as pl
from jax.experimental.pallas import tpu as pltpu

F32 = jnp.float32
BF16 = jnp.bfloat16

D_MODEL = 1024
D_CONV = 512
CONV_K = 31
N_HEADS = 8
N_KV = 2
HEAD_DIM = 64
GROUP = 4
Q_W = N_HEADS * HEAD_DIM
KV_W = N_KV * HEAD_DIM
BLOCK = 64
N_SEL = 16
WINDOW = 512
ROPE_THETA = 10000.0
D_FF = 2816
EPS = 1e-6
FORCED_BONUS = 2.0 * GROUP
PAGE = 128
OFF_Q = 2 * D_CONV
OFF_KV = OFF_Q + Q_W
OFF_NG = OFF_KV + 6 * KV_W
OFF_MG = OFF_NG + 3 * N_HEADS
Q_SCALE = HEAD_DIM ** -0.5 * math.log2(math.e)
NEG = -(2.0 ** 100)
M_INIT = -(2.0 ** 99)
ONES_ROWS = 16
V_AUG = HEAD_DIM + ONES_ROWS
K_AUG = KV_W + 16
N_T_ROWS = 1312
CONV_HALO = 32
SUBLANES = 8
LANES = 128
SEL_CHUNK = 512
VMEM_LIMIT = 56 * 1024 * 1024


def _cparams(sem):
    return pltpu.CompilerParams(dimension_semantics=sem, vmem_limit_bytes=VMEM_LIMIT)


def _ada_kernel(c_ref, w_ref, b_ref, o_ref):
    c = c_ref[...]
    s = c * jax.nn.sigmoid(c)
    o_ref[...] = jnp.dot(s.astype(BF16), w_ref[...].astype(BF16), preferred_element_type=F32) + b_ref[...]


def _ada(c, w_ada, b_ada):
    n, tn = c.shape[0], 1024
    return pl.pallas_call(
        _ada_kernel,
        grid=(w_ada.shape[1] // tn,),
        in_specs=[pl.BlockSpec((n, D_MODEL), lambda j: (0, 0)),
                  pl.BlockSpec((D_MODEL, tn), lambda j: (0, j)),
                  pl.BlockSpec((1, tn), lambda j: (0, j))],
        out_specs=pl.BlockSpec((n, tn), lambda j: (0, j)),
        out_shape=jax.ShapeDtypeStruct((n, w_ada.shape[1]), F32),
        compiler_params=_cparams(("arbitrary",)),
        name="ada",
    )(c, w_ada, b_ada.reshape(1, -1))


def _front_kernel(x_ref, sh1_ref, sc1_ref, g_ref, wn_ref, wt_ref, gq_ref, gk_ref, cos_ref, sin_ref,
                  u_ref, gmg_ref, qp_ref, qr_ref, kcmpT_ref, vcmpT_ref, kselT_ref, vselT_ref, kwinT_ref, vwinT_ref,
                  kselb_ref, kwinb_ref, vselTb_ref, vwinTb_ref, ngT_ref):
    x = x_ref[0]
    ms = jnp.mean(x * x, axis=-1, keepdims=True)
    h = x * lax.rsqrt(ms + EPS) * g_ref[...]
    h = h * (1.0 + sc1_ref[0]) + sh1_ref[0]
    hb = h.astype(BF16)
    zn = jnp.dot(hb, wn_ref[...], preferred_element_type=F32)
    u_ref[0] = zn[:, :D_CONV] * jax.nn.sigmoid(zn[:, D_CONV:2 * D_CONV])
    gmg_ref[0] = jax.nn.sigmoid(zn[:, 2 * D_CONV:])
    zt = lax.dot_general(wt_ref[...], hb, (((1,), (1,)), ((), ())), preferred_element_type=F32)
    cos = cos_ref[...]
    sin = sin_ref[...]
    half = HEAD_DIM // 2

    def norm(v, g):
        m = jnp.mean(v * v, axis=0, keepdims=True)
        return v * lax.rsqrt(m + EPS) * g

    def rope(v):
        a, b = v[:half], v[half:]
        return jnp.concatenate([a * cos - b * sin, b * cos + a * sin], axis=0)

    gq = gq_ref[...]
    for hh in range(N_HEADS):
        q = norm(zt[hh * HEAD_DIM:(hh + 1) * HEAD_DIM], gq) * Q_SCALE
        qp_ref[0, hh * HEAD_DIM:(hh + 1) * HEAD_DIM, :] = q.astype(BF16)
        qr_ref[0, hh * HEAD_DIM:(hh + 1) * HEAD_DIM, :] = rope(q).astype(BF16)

    def head_pair(j, g=None, rotary=False):
        outs = []
        for hh in range(N_KV):
            lo = Q_W + j * KV_W + hh * HEAD_DIM
            v = zt[lo:lo + HEAD_DIM]
            if g is not None:
                v = norm(v, g)
            if rotary:
                v = rope(v)
            outs.append(v)
        return jnp.concatenate(outs, axis=0)

    k_sel = head_pair(2, gk_ref[1], True)
    v_sel = head_pair(3)
    k_win = head_pair(4, gk_ref[2], True)
    v_win = head_pair(5)
    kcmpT_ref[0] = head_pair(0, gk_ref[0])
    vcmpT_ref[0] = head_pair(1)
    kselT_ref[0] = k_sel
    vselT_ref[0] = v_sel
    kwinT_ref[0] = k_win
    vwinT_ref[0] = v_win
    tm = x.shape[0]
    blk = lax.shift_right_logical(pl.program_id(1) * tm + lax.broadcasted_iota(jnp.int32, (tm, K_AUG - KV_W), 0), 6)
    onehot = (blk % (SEL_CHUNK // BLOCK)) == lax.broadcasted_iota(jnp.int32, (tm, K_AUG - KV_W), 1)
    kselb_ref[0] = jnp.concatenate([k_sel.T, jnp.where(onehot, 1.0, 0.0)], axis=1).astype(BF16)
    kwinb_ref[0] = k_win.T.astype(BF16)
    ones = jnp.ones((ONES_ROWS, tm), F32)

    def v_aug(v):
        return jnp.concatenate([v[:HEAD_DIM], ones, v[HEAD_DIM:], ones], axis=0).astype(BF16)

    vselTb_ref[0] = v_aug(v_sel)
    vwinTb_ref[0] = v_aug(v_win)
    ngT_ref[0] = jax.nn.sigmoid(zt[Q_W + 6 * KV_W:])


def _front(x, mod, pos, prm, tm):
    B, T, _ = x.shape
    Tm = mod.shape[1]
    tmm = 1 if Tm == 1 else tm
    half = HEAD_DIM // 2
    inv = jnp.power(ROPE_THETA, -jnp.arange(half, dtype=F32) / half)
    ang = pos.astype(F32)[:, None] * inv[None, :]
    cosT, sinT = jnp.cos(ang).T, jnp.sin(ang).T
    gq = jnp.broadcast_to(prm["q_norm_g"][:, None], (HEAD_DIM, tm))
    gk = jnp.broadcast_to(prm["k_norm_g"][:, :, None], (3, HEAD_DIM, tm))
    nt = T // tm
    tok = lambda w: pl.BlockSpec((1, tm, w), lambda b, t: (b, t, 0))
    tokT = lambda r: pl.BlockSpec((1, r, tm), lambda b, t: (b, 0, t))
    modspec = lambda k: pl.BlockSpec((1, tmm, D_MODEL), (lambda b, t: (b, t, k)) if Tm != 1 else (lambda b, t: (b, 0, k)))
    const = lambda shape: pl.BlockSpec(shape, lambda b, t: (0,) * len(shape))
    sd = jax.ShapeDtypeStruct
    outs = pl.pallas_call(
        _front_kernel,
        grid=(B, nt),
        in_specs=[tok(D_MODEL), modspec(0), modspec(1), const((1, D_MODEL)),
                  const((D_MODEL, 3 * D_MODEL)), const((N_T_ROWS, D_MODEL)),
                  const((HEAD_DIM, tm)), const((3, HEAD_DIM, tm)),
                  pl.BlockSpec((half, tm), lambda b, t: (0, t)), pl.BlockSpec((half, tm), lambda b, t: (0, t))],
        out_specs=[tok(D_CONV), tok(2 * D_MODEL), tokT(Q_W), tokT(Q_W)] + [tokT(KV_W)] * 6
                  + [tok(K_AUG), tok(KV_W), tokT(N_KV * V_AUG), tokT(N_KV * V_AUG), tokT(32)],
        out_shape=[sd((B, T, D_CONV), F32), sd((B, T, 2 * D_MODEL), F32), sd((B, Q_W, T), BF16), sd((B, Q_W, T), BF16)]
                  + [sd((B, KV_W, T), F32)] * 6
                  + [sd((B, T, K_AUG), BF16), sd((B, T, KV_W), BF16), sd((B, N_KV * V_AUG, T), BF16),
                     sd((B, N_KV * V_AUG, T), BF16), sd((B, 32, T), F32)],
        compiler_params=_cparams(("parallel", "arbitrary")),
        name="front",
    )(x, mod, mod, prm["mix_norm_g"].reshape(1, -1), prm["wn"], prm["wt"], gq, gk, cosT, sinT)
    names = ("u", "gmg", "qp", "qr", "kcmpT", "vcmpT", "kselT", "vselT", "kwinT", "vwinT",
             "kselb", "kwinb", "vselTb", "vwinTb", "ngT")
    return dict(zip(names, outs))


def _conv_kernel(u_ref, st_ref, wdw_ref, bdw_ref, lng_ref, lnb_ref, wpw_ref, uo_ref, so_ref, ext_ref, sh_ref, *, tm):
    t = pl.program_id(1)
    pad = CONV_HALO - (CONV_K - 1)
    n_sh = tm + CONV_HALO - SUBLANES

    @pl.when(t == 0)
    def _():
        ext_ref[0:pad, :] = jnp.zeros((pad, D_CONV), F32)
        ext_ref[pad:CONV_HALO, :] = st_ref[0]

    ext_ref[CONV_HALO:CONV_HALO + tm, :] = u_ref[0]
    for r in range(1, SUBLANES):
        sh_ref[r - 1] = ext_ref[r:r + n_sh, :]
    y = jnp.zeros((tm, D_CONV), F32) + bdw_ref[...]
    for k in range(CONV_K):
        a, r = divmod(pad + k, SUBLANES)
        rows = ext_ref[SUBLANES * a:SUBLANES * a + tm, :] if r == 0 else sh_ref[r - 1, SUBLANES * a:SUBLANES * a + tm, :]
        y = y + rows * wdw_ref[k:k + 1, :]
    mu = jnp.mean(y, axis=-1, keepdims=True)
    yc = y - mu
    var = jnp.mean(yc * yc, axis=-1, keepdims=True)
    yn = yc * lax.rsqrt(var + EPS) * lng_ref[...] + lnb_ref[...]
    ya = yn * jax.nn.sigmoid(yn)
    uo_ref[0] = jnp.dot(ya.astype(BF16), wpw_ref[...], preferred_element_type=F32)
    so_ref[0] = ext_ref[tm + pad:tm + CONV_HALO, :]
    carry = ext_ref[tm:tm + CONV_HALO, :]
    ext_ref[0:CONV_HALO, :] = carry


def _conv(u, state, prm, tm):
    B, T, _ = u.shape
    const = lambda shape: pl.BlockSpec(shape, lambda b, t: (0,) * len(shape))
    return pl.pallas_call(
        functools.partial(_conv_kernel, tm=tm),
        grid=(B, T // tm),
        in_specs=[pl.BlockSpec((1, tm, D_CONV), lambda b, t: (b, t, 0)),
                  pl.BlockSpec((1, CONV_K - 1, D_CONV), lambda b, t: (b, 0, 0)),
                  const((CONV_K, D_CONV)), const((1, D_CONV)), const((1, D_CONV)), const((1, D_CONV)),
                  const((D_CONV, D_MODEL))],
        out_specs=[pl.BlockSpec((1, tm, D_MODEL), lambda b, t: (b, t, 0)),
                   pl.BlockSpec((1, CONV_K - 1, D_CONV), lambda b, t: (b, 0, 0))],
        out_shape=[jax.ShapeDtypeStruct((B, T, D_MODEL), F32), jax.ShapeDtypeStruct((B, CONV_K - 1, D_CONV), F32)],
        scratch_shapes=[pltpu.VMEM((CONV_HALO + tm, D_CONV), F32),
                        pltpu.VMEM((SUBLANES - 1, tm + CONV_HALO - SUBLANES, D_CONV), F32)],
        compiler_params=_cparams(("parallel", "arbitrary")),
        name="conv",
    )(u, state, prm["w_dw"], prm["b_dw"].reshape(1, -1), prm["conv_ln_g"].reshape(1, -1),
      prm["conv_ln_b"].reshape(1, -1), prm["w_pw2b"])


def _gather_pages(pt_ref, srcs, buf_ref, sem_ref, pages_per_step):
    step = pl.program_id(0) * pl.num_programs(1) + pl.program_id(1)
    n_steps = pl.num_programs(0) * pl.num_programs(1)
    slot = step % 2

    def copies(s, slot_):
        return [pltpu.make_async_copy(src.at[pt_ref[s * pages_per_step + i]], buf_ref.at[slot_, a, i], sem_ref.at[slot_])
                for a, src in enumerate(srcs) for i in range(pages_per_step)]

    @pl.when(step == 0)
    def _():
        for n, c in enumerate(copies(step, slot)):
            c.start(priority=n % 2)

    @pl.when(step + 1 < n_steps)
    def _():
        for n, c in enumerate(copies(step + 1, 1 - slot)):
            c.start(priority=n % 2)

    for c in copies(step, slot):
        c.wait()
    return [jnp.concatenate([buf_ref[slot, a, i] for i in range(pages_per_step)], axis=1) for a in range(len(srcs))]


def _compress_core(xk, xv, modk_ref, wk_ref, modv_ref, wv_ref, s_ref, kc_ref, vc_ref, acc_ref, group):
    slot = pl.program_id(1) % group

    @pl.when(slot == 0)
    def _():
        acc_ref[...] = jnp.zeros_like(acc_ref)

    place = s_ref[slot]
    rows = [(x * (1.0 + mod_ref[...])).astype(BF16) for x, mod_ref in ((xk, modk_ref), (xv, modv_ref))]
    sums = jnp.dot(jnp.concatenate(rows, axis=0), place, preferred_element_type=F32)
    for i, (w_ref, out_ref) in enumerate(((wk_ref, kc_ref), (wv_ref, vc_ref))):
        acc_ref[i] += sums[i * KV_W:(i + 1) * KV_W]
        out_ref[0] = jnp.dot(w_ref[...], acc_ref[i].astype(BF16), preferred_element_type=F32)


def _compress_kernel(k_ref, v_ref, *refs, group):
    _compress_core(k_ref[0], v_ref[0], *refs, group=group)


def _compress_paged_kernel(pt_ref, k_hbm, v_hbm, *refs, group, pages_per_step):
    *core, buf_ref, sem_ref = refs
    xk, xv = _gather_pages(pt_ref, (k_hbm, v_hbm), buf_ref, sem_ref, pages_per_step)
    _compress_core(xk, xv, *core, group=group)


def _compress_consts(prm, cols):
    nbs = max(cols // BLOCK, 1)
    group = LANES // nbs
    modT = lambda m: jnp.tile(m.T, (N_KV, max(cols // BLOCK, 1)))[:, :cols]
    wT = lambda w: jnp.kron(jnp.eye(N_KV, dtype=F32), w.T).astype(BF16)
    r = jnp.arange(cols)[None, :, None]
    g = jnp.arange(group)[:, None, None]
    lane = jnp.arange(LANES)[None, None, :]
    place = jnp.where(lane == g * nbs + r // BLOCK, 1.0 / BLOCK, 0.0).astype(BF16)
    return (modT(prm["cmp_mod_k"]), wT(prm["cmp_w_k"]), modT(prm["cmp_mod_v"]), wT(prm["cmp_w_v"]), place), group


def _compress_specs(cols, group, n_batch, n_out_blocks):
    const = lambda shape: pl.BlockSpec(shape, lambda *a: (0,) * len(shape))
    ospec = pl.BlockSpec((1, KV_W, LANES), lambda b, j, *a: (b, 0, j // group))
    osh = jax.ShapeDtypeStruct((n_batch, KV_W, n_out_blocks * LANES), F32)
    const_specs = [const((KV_W, cols)), const((KV_W, KV_W)), const((KV_W, cols)), const((KV_W, KV_W)),
                   const((group, cols, LANES))]
    return const_specs, [ospec, ospec], [osh, osh], [pltpu.VMEM((2, KV_W, LANES), F32)]


def _compress(kT, vT, prm, cols):
    B, _, L = kT.shape
    cols = min(cols, L)
    n_blocks = -(-(L // BLOCK) // LANES) if L >= BLOCK else 1
    consts, group = _compress_consts(prm, cols)
    const_specs, out_specs, out_shape, scratch = _compress_specs(cols, group, B, n_blocks)
    x_spec = pl.BlockSpec((1, KV_W, cols), lambda b, j: (b, 0, j))
    return pl.pallas_call(
        functools.partial(_compress_kernel, group=group), grid=(B, L // cols),
        in_specs=[x_spec, x_spec] + const_specs, out_specs=out_specs, out_shape=out_shape, scratch_shapes=scratch,
        compiler_params=_cparams(("parallel", "arbitrary")), name="compress",
    )(kT, vT, *consts)


def _compress_paged(cache_kT, cache_vT, page_table, prm, pages_per_step):
    B, n_pages = page_table.shape
    cols = pages_per_step * PAGE
    n_blocks = -(-(n_pages * PAGE // BLOCK) // LANES)
    consts, group = _compress_consts(prm, cols)
    const_specs, out_specs, out_shape, scratch = _compress_specs(cols, group, B, n_blocks)
    hbm = pl.BlockSpec(memory_space=pl.ANY)
    grid_spec = pltpu.PrefetchScalarGridSpec(
        num_scalar_prefetch=1, grid=(B, n_pages // pages_per_step),
        in_specs=[hbm, hbm] + const_specs, out_specs=out_specs,
        scratch_shapes=scratch + [pltpu.VMEM((2, 2, pages_per_step, KV_W, PAGE), F32), pltpu.SemaphoreType.DMA((2,))])
    return pl.pallas_call(
        functools.partial(_compress_paged_kernel, group=group, pages_per_step=pages_per_step),
        grid_spec=grid_spec, out_shape=out_shape,
        compiler_params=_cparams(("arbitrary", "arbitrary")), name="compress_paged",
    )(page_table.reshape(-1), cache_kT, cache_vT, *consts)


def _attend_scores(s, vT, mask, carry, s_max=None):
    m, acc = carry
    if mask is not None:
        s = jnp.where(mask, s, NEG)
    if s_max is None:
        s_max = jnp.max(s, axis=0, keepdims=True)
    m_new = jnp.maximum(m, s_max)
    alpha = jnp.exp2(m - m_new)
    p = jnp.exp2(s - m_new).astype(BF16)
    return m_new, alpha * acc + jnp.dot(vT, p, preferred_element_type=F32)


def _attend(qT, k, vT, mask, carry):
    return _attend_scores(jnp.dot(k, qT, preferred_element_type=F32), vT, mask, carry)


def _attend_init(L, rows=V_AUG):
    return jnp.full((1, L), M_INIT, F32), jnp.zeros((rows, L), F32)


def _attend_out(carry):
    _, acc = carry
    return acc[:HEAD_DIM] / jnp.maximum(acc[HEAD_DIM:HEAD_DIM + 1], 1e-30)


def _with_ones(vT):
    return jnp.concatenate([vT, jnp.ones((ONES_ROWS, vT.shape[1]), BF16)], axis=0)


def _block_mask(sel_rows, kc):
    nb, L = sel_rows.shape
    return jnp.broadcast_to((sel_rows > 0.5)[:, None, :], (nb, BLOCK, L)).reshape(kc, L)


def _cmp_and_select(qpT, kc, vcT, q0, tq):
    return _cmp_finish(jnp.dot(kc, qpT, preferred_element_type=F32), vcT, q0, tq)


def _cmp_finish(s, vcT, q0, tq):
    nb, L = s.shape
    blk = lax.broadcasted_iota(jnp.int32, (nb, L), 0)
    qpos = q0 + lax.broadcasted_iota(jnp.int32, (1, L), 1) % tq
    ok = (blk + 1) * BLOCK <= qpos + 1
    s = jnp.where(ok, s, -jnp.inf)
    m = jnp.max(s, axis=0, keepdims=True)
    m = jnp.where(m > -jnp.inf, m, 0.0)
    e = jnp.where(ok, jnp.exp2(s - m), 0.0)
    p = e / jnp.maximum(jnp.sum(e, axis=0, keepdims=True), 1e-30)
    o_cmp = jnp.dot(vcT, p.astype(BF16), preferred_element_type=F32)
    if tq % LANES == 0:
        w = tq
        imp = p[:, :tq]
        for g in range(1, GROUP):
            imp = imp + p[:, g * tq:(g + 1) * tq]
    else:
        w = L
        seg = GROUP * tq
        in_seg = lax.broadcasted_iota(jnp.int32, (1, L), 1) % seg
        imp = p
        for g in range(1, GROUP):
            rolled = pltpu.roll(p, g * tq, 1)
            if seg != L:
                rolled = jnp.where(in_seg >= g * tq, rolled, pltpu.roll(p, (g * tq - seg) % L, 1))
            imp = imp + rolled
    if w != L:
        blk = lax.broadcasted_iota(jnp.int32, (nb, w), 0)
        qpos = q0 + lax.broadcasted_iota(jnp.int32, (1, w), 1)
    cur = lax.shift_right_logical(qpos, 6)
    visible = blk <= cur
    forced = (blk == 0) | (blk == cur) | (blk == cur - 1)
    score = jnp.where(visible, imp + jnp.where(forced, FORCED_BONUS, 0.0), -1.0)
    ridx = blk.astype(F32)
    sel = jnp.zeros_like(score)
    rem = score
    for _ in range(min(N_SEL, nb)):
        mx = jnp.max(rem, axis=0, keepdims=True)
        first = jnp.min(jnp.where(rem == mx, ridx, float(nb)), axis=0, keepdims=True)
        pick = ridx == first
        rem = jnp.where(pick, -jnp.inf, rem)
        sel = jnp.where(pick, 1.0, sel)
    if w != L:
        sel = jnp.concatenate([sel] * GROUP, axis=1)
    return o_cmp, sel


def _nsa_prompt_kernel(qp_ref, qr_ref, kc_ref, vcT_ref, ksel_ref, vselT_ref, kwin_ref, vwinT_ref, ngT_ref,
                       o_ref, bias_ref, s_ref, smax_ref, sel_ref, win_ref, diag_ref, *, tq):
    t = pl.program_id(1)
    s0 = t * tq
    L = GROUP * tq
    lane = lax.broadcasted_iota(jnp.int32, (1, L), 1)
    qpos = s0 + lane % tq
    ng = ngT_ref[0]
    heads = [slice(h * HEAD_DIM, (h + 1) * HEAD_DIM) for h in range(N_KV)]

    def qpad(ref, h):
        q = jnp.concatenate([ref[0, (h * GROUP + g) * HEAD_DIM:(h * GROUP + g + 1) * HEAD_DIM, :]
                             for g in range(GROUP)], axis=1)
        z = jnp.zeros_like(q)
        return jnp.concatenate([q, z] if h == 0 else [z, q], axis=0)

    qr = [qpad(qr_ref, h) for h in range(N_KV)]
    vrows = [slice(h * V_AUG, (h + 1) * V_AUG) for h in range(N_KV)]
    n_blk = SEL_CHUNK // BLOCK
    n_full = s0 // SEL_CHUNK
    k_diag = pl.multiple_of(n_full * SEL_CHUNK, SEL_CHUNK)
    w0 = pl.multiple_of(jnp.maximum(s0 - WINDOW, 0), tq)

    s_cmp = [jnp.dot(kc_ref[0], qpad(qp_ref, h), preferred_element_type=F32) for h in range(N_KV)]
    for h in range(N_KV):
        win_ref[h] = jnp.dot(kwin_ref[0, pl.ds(w0, WINDOW + tq), :], qr[h], preferred_element_type=F32)
        diag_ref[h] = jnp.dot(ksel_ref[0, pl.ds(k_diag, SEL_CHUNK), pl.ds(0, KV_W)], qr[h], preferred_element_type=F32)

    o_cmp = []
    for h in range(N_KV):
        o, sel = _cmp_finish(s_cmp[h], vcT_ref[0, heads[h], :], s0, tq)
        o_cmp.append(o)
        sel_ref[h] = sel
        bias = jnp.where(sel > 0.5, 0.0, NEG).reshape(-1, n_blk, L)
        bias = jnp.concatenate([bias, jnp.full((1, n_blk, L), NEG, F32)], axis=0)
        bias_ref[h] = jnp.concatenate([bias, jnp.zeros_like(bias)], axis=1).astype(BF16)

    dead = bias_ref.shape[1] - 1

    def scores(c, bias_idx, h):
        k0 = pl.multiple_of(c * SEL_CHUNK, SEL_CHUNK)
        return jnp.dot(ksel_ref[0, pl.ds(k0, SEL_CHUNK), :], jnp.concatenate([qr[h], bias_ref[h, bias_idx]], axis=0),
                       preferred_element_type=F32)

    def issue(c, slot):
        c_eff = jnp.minimum(c, jnp.maximum(n_full - 1, 0))
        for h in range(N_KV):
            s = scores(c_eff, jnp.where(c < n_full, c, dead), h)
            s_ref[slot, h] = s
            smax_ref[slot, h] = jnp.max(s, axis=0, keepdims=True)

    def consume(c, slot, carries):
        k0 = pl.multiple_of(jnp.minimum(c, jnp.maximum(n_full - 1, 0)) * SEL_CHUNK, SEL_CHUNK)
        return tuple(_attend_scores(s_ref[slot, h], vselT_ref[0, vrows[h], pl.ds(k0, SEL_CHUNK)], None, carries[h],
                                    smax_ref[slot, h]) for h in range(N_KV))

    issue(0, 0)

    kpos = w0 + lax.broadcasted_iota(jnp.int32, (WINDOW + tq, L), 0)
    wmask = (kpos <= qpos) & (kpos > qpos - WINDOW)
    o_win = [_attend_out(_attend_scores(win_ref[h], vwinT_ref[0, vrows[h], pl.ds(w0, WINDOW + tq)], wmask, _attend_init(L)))
             for h in range(N_KV)]

    causal = k_diag + lax.broadcasted_iota(jnp.int32, (SEL_CHUNK, L), 0) <= qpos
    r_diag = pl.multiple_of(n_full * n_blk, n_blk)
    carries = tuple(_attend_scores(diag_ref[h], vselT_ref[0, vrows[h], pl.ds(k_diag, SEL_CHUNK)],
                                   _block_mask(sel_ref[h, pl.ds(r_diag, n_blk), :], SEL_CHUNK) & causal, _attend_init(L))
                    for h in range(N_KV))

    def pair(i, carries):
        c = 2 * i
        issue(c + 1, 1)
        carries = consume(c, 0, carries)
        issue(c + 2, 0)
        return consume(c + 1, 1, carries)

    carries = lax.fori_loop(0, (n_full + 1) // 2, pair, carries)

    def gate(h, br):
        return jnp.concatenate([ng[h * 12 + g * 3 + br:h * 12 + g * 3 + br + 1, :] for g in range(GROUP)], axis=1)

    for h in range(N_KV):
        o_sel = _attend_out(carries[h])
        o = gate(h, 0) * o_cmp[h] + gate(h, 1) * o_sel + gate(h, 2) * o_win[h]
        oT = jnp.concatenate([o[:, g * tq:(g + 1) * tq] for g in range(GROUP)], axis=0)
        o_ref[0, :, h * GROUP * HEAD_DIM:(h + 1) * GROUP * HEAD_DIM] = oT.T.astype(o_ref.dtype)


def _nsa_prompt(f, kcb, vcTb, tq):
    B, _, T = f["qp"].shape
    assert T % SEL_CHUNK == 0 and SEL_CHUNK % tq == 0 and WINDOW % tq == 0 and T >= WINDOW + tq
    nbp = kcb.shape[1]
    full = lambda shape: pl.BlockSpec((1,) + shape, lambda b, t: (b, 0, 0))
    return pl.pallas_call(
        functools.partial(_nsa_prompt_kernel, tq=tq),
        grid=(B, T // tq),
        in_specs=[pl.BlockSpec((1, Q_W, tq), lambda b, t: (b, 0, t)), pl.BlockSpec((1, Q_W, tq), lambda b, t: (b, 0, t)),
                  full((nbp, KV_W)), full((KV_W, nbp)),
                  full((T, K_AUG)), full((N_KV * V_AUG, T)), full((T, KV_W)), full((N_KV * V_AUG, T)),
                  pl.BlockSpec((1, 32, tq), lambda b, t: (b, 0, t))],
        out_specs=pl.BlockSpec((1, tq, Q_W), lambda b, t: (b, t, 0)),
        out_shape=jax.ShapeDtypeStruct((B, T, Q_W), BF16),
        scratch_shapes=[pltpu.VMEM((N_KV, nbp // (SEL_CHUNK // BLOCK) + 1, 2 * (SEL_CHUNK // BLOCK), GROUP * tq), BF16),
                        pltpu.VMEM((2, N_KV, SEL_CHUNK, GROUP * tq), F32), pltpu.VMEM((2, N_KV, 1, GROUP * tq), F32),
                        pltpu.VMEM((N_KV, nbp, GROUP * tq), F32),
                        pltpu.VMEM((N_KV, WINDOW + tq, GROUP * tq), F32),
                        pltpu.VMEM((N_KV, SEL_CHUNK, GROUP * tq), F32)],
        compiler_params=_cparams(("parallel", "arbitrary")),
        name="nsa_prompt",
    )(f["qp"], f["qr"], kcb, vcTb, f["kselb"], f["vselTb"], f["kwinb"], f["vwinTb"], f["ngT"])


def _two_head_values(vT):
    ones = jnp.ones((ONES_ROWS, vT.shape[1]), BF16)
    return jnp.concatenate([vT[:HEAD_DIM], ones, vT[HEAD_DIM:], ones], axis=0)


def _own_head_rows(x, lane_head):
    r = x.shape[0] // N_KV
    return jnp.where(lane_head == 0, x[:r], x[r:])


def _nsa_sample_kernel(pt_ref, qp_ref, qr_ref, kc_ref, vcT_ref, gate_ref, k_hbm, v_hbm,
                       knew_ref, vnewT_ref, cwkT_ref, cwvT_ref, kwnew_ref, vwnewT_ref,
                       o_ref, sel_ref, ocmp_ref, m_ref, acc_ref, buf_ref, sem_ref, *, tqs, past_len, pages_per_step):
    kT_pages, vT_pages = _gather_pages(pt_ref, (k_hbm, v_hbm), buf_ref, sem_ref, pages_per_step)
    j = pl.program_id(1)
    n_steps = pl.num_programs(1)
    L = N_KV * GROUP * tqs
    kc_step = pages_per_step * PAGE
    lane = lax.broadcasted_iota(jnp.int32, (1, L), 1)
    qi = lane % tqs
    lane_head = lane // (GROUP * tqs)

    def out(carry):
        acc = _own_head_rows(carry[1], lane_head)
        return acc[:HEAD_DIM] / jnp.maximum(acc[HEAD_DIM:HEAD_DIM + 1], 1e-30)

    @pl.when(j == 0)
    def _():
        o_cmp, sel = _cmp_and_select(qp_ref[0], kc_ref[0], vcT_ref[0], past_len, tqs)
        sel_ref[...] = sel
        ocmp_ref[...] = _own_head_rows(o_cmp, lane_head)
        m_ref[...], acc_ref[...] = _attend_init(L, N_KV * V_AUG)

    k = kT_pages.T.astype(BF16)
    r0 = pl.multiple_of(j * (kc_step // BLOCK), kc_step // BLOCK)
    mask = _block_mask(sel_ref[pl.ds(r0, kc_step // BLOCK), :], kc_step)
    vT = _two_head_values(vT_pages.astype(BF16))
    half = kc_step // 2
    scores = [jnp.dot(k[i * half:(i + 1) * half], qr_ref[0], preferred_element_type=F32) for i in range(2)]
    carry = (m_ref[...], acc_ref[...])
    for i in range(2):
        carry = _attend_scores(scores[i], vT[:, i * half:(i + 1) * half], mask[i * half:(i + 1) * half], carry)
    m_ref[...], acc_ref[...] = carry

    @pl.when(j == n_steps - 1)
    def _():
        nb_past = past_len // BLOCK
        buf = cwkT_ref.shape[2]
        r_new = lax.broadcasted_iota(jnp.int32, (tqs, L), 0)
        r_buf = lax.broadcasted_iota(jnp.int32, (buf, L), 0)
        qr = qr_ref[0]
        mask_new = (sel_ref[nb_past:nb_past + 1, :] > 0.5) & (r_new <= qi)
        o_sel = out(_attend(qr, knew_ref[0], _two_head_values(vnewT_ref[0]), mask_new, (m_ref[...], acc_ref[...])))
        carry = _attend(qr, cwkT_ref[0].T.astype(BF16), _two_head_values(cwvT_ref[0].astype(BF16)),
                        r_buf > qi + (buf - WINDOW), _attend_init(L, N_KV * V_AUG))
        o_win = out(_attend(qr, kwnew_ref[0], _two_head_values(vwnewT_ref[0]), r_new <= qi, carry))
        g = gate_ref[0]
        o_ref[0] = g[0:1] * ocmp_ref[...] + g[1:2] * o_sel + g[2:3] * o_win


def _nsa_sample(qpT, qrT, kcb, vcTb, gates, cache_sel_kT, cache_sel_vT, page_table, knew, vnewT,
                cache_win_kT, cache_win_vT, kwnew, vwnewT, tqs, pages_per_step):
    Bs, n_pages = page_table.shape
    L = N_KV * GROUP * tqs
    nbp = kcb.shape[1]
    buf = cache_win_kT.shape[2]
    per_b = lambda shape: pl.BlockSpec((1,) + shape, lambda b, j, pt: (b,) + (0,) * len(shape))
    hbm = pl.BlockSpec(memory_space=pl.ANY)
    grid_spec = pltpu.PrefetchScalarGridSpec(
        num_scalar_prefetch=1,
        grid=(Bs, n_pages // pages_per_step),
        in_specs=[per_b((KV_W, L)), per_b((KV_W, L)), per_b((nbp, KV_W)), per_b((KV_W, nbp)), per_b((8, L))]
                 + [hbm, hbm]
                 + [per_b((tqs, KV_W)), per_b((KV_W, tqs)), per_b((KV_W, buf)), per_b((KV_W, buf)),
                    per_b((tqs, KV_W)), per_b((KV_W, tqs))],
        out_specs=per_b((HEAD_DIM, L)),
        scratch_shapes=[pltpu.VMEM((nbp, L), F32), pltpu.VMEM((HEAD_DIM, L), F32),
                        pltpu.VMEM((1, L), F32), pltpu.VMEM((N_KV * V_AUG, L), F32),
                        pltpu.VMEM((2, 2, pages_per_step, KV_W, PAGE), F32), pltpu.SemaphoreType.DMA((2,))],
    )
    return pl.pallas_call(
        functools.partial(_nsa_sample_kernel, tqs=tqs, past_len=n_pages * PAGE, pages_per_step=pages_per_step),
        grid_spec=grid_spec,
        out_shape=jax.ShapeDtypeStruct((Bs, HEAD_DIM, L), F32),
        compiler_params=_cparams(("arbitrary", "arbitrary")),
        name="nsa_sample",
    )(page_table.reshape(-1), qpT, qrT, kcb, vcTb, gates, cache_sel_kT, cache_sel_vT,
      knew, vnewT, cache_win_kT, cache_win_vT, kwnew, vwnewT)


def _back_kernel(x_ref, o_ref, uo_ref, gmg_ref, g1_ref, sh2_ref, sc2_ref, g2_ref, fg_ref,
                 wno_ref, wout_ref, wg_ref, wu_ref, wd_ref, y_ref):
    nsa = jnp.dot(o_ref[0], wno_ref[...], preferred_element_type=F32)
    gmg = gmg_ref[0]
    merged = gmg[:, :D_MODEL] * uo_ref[0] + gmg[:, D_MODEL:] * nsa
    mix = jnp.dot(merged.astype(BF16), wout_ref[...], preferred_element_type=F32)
    x1 = x_ref[0] + g1_ref[0] * mix
    ms = jnp.mean(x1 * x1, axis=-1, keepdims=True)
    h = x1 * lax.rsqrt(ms + EPS) * fg_ref[...]
    h = h * (1.0 + sc2_ref[0]) + sh2_ref[0]
    hb = h.astype(BF16)
    a = jnp.dot(hb, wg_ref[...], preferred_element_type=F32)
    b = jnp.dot(hb, wu_ref[...], preferred_element_type=F32)
    act = a * jax.nn.sigmoid(a) * b
    ffn = jnp.dot(act.astype(BF16), wd_ref[...], preferred_element_type=F32)
    y_ref[0] = x1 + g2_ref[0] * ffn


def _back(x, o, u_out, gmg, mod, prm, tm):
    B, T, _ = x.shape
    Tm = mod.shape[1]
    tmm = 1 if Tm == 1 else tm
    tok = lambda w: pl.BlockSpec((1, tm, w), lambda b, t: (b, t, 0))
    modspec = lambda k: pl.BlockSpec((1, tmm, D_MODEL), (lambda b, t: (b, t, k)) if Tm != 1 else (lambda b, t: (b, 0, k)))
    wspec = lambda shape: pl.BlockSpec(shape, lambda b, t: (0, 0), pipeline_mode=pl.Buffered(1))
    return pl.pallas_call(
        _back_kernel,
        grid=(B, T // tm),
        in_specs=[tok(D_MODEL), tok(Q_W), tok(D_MODEL), tok(2 * D_MODEL),
                  modspec(2), modspec(3), modspec(4), modspec(5), wspec((1, D_MODEL)),
                  wspec((Q_W, D_MODEL)), wspec((D_MODEL, D_MODEL)), wspec((D_MODEL, D_FF)), wspec((D_MODEL, D_FF)),
                  wspec((D_FF, D_MODEL))],
        out_specs=tok(D_MODEL),
        out_shape=jax.ShapeDtypeStruct((B, T, D_MODEL), F32),
        compiler_params=_cparams(("parallel", "arbitrary")),
        name="back",
    )(x, o, u_out, gmg, mod, mod, mod, mod, prm["ffn_norm_g"].reshape(1, -1),
      prm["w_nsa_ob"], prm["w_outb"], prm["w_gateb"], prm["w_upb"], prm["w_downb"])


def _prep_params(p):
    w_in = p["w_in"]
    q = dict(p)
    q["wn"] = jnp.concatenate([w_in[:, :OFF_Q], w_in[:, OFF_MG:]], axis=1).astype(BF16)
    wt = jnp.concatenate([w_in[:, OFF_Q:OFF_MG], jnp.zeros((D_MODEL, N_T_ROWS - (OFF_MG - OFF_Q)), F32)], axis=1)
    q["wt"] = wt.T.astype(BF16)
    q["w_pw2b"] = p["w_pw2"].astype(BF16)
    for name in ("w_nsa_o", "w_out", "w_gate", "w_up", "w_down"):
        q[name + "b"] = p[name].astype(BF16)
    return q


def _rows_from_channel_major(a):
    B, _, T = a.shape
    return jnp.transpose(a.reshape(B, N_KV, HEAD_DIM, T), (0, 3, 1, 2))


def _channel_major(a):
    N, R = a.shape[:2]
    return jnp.transpose(a, (0, 2, 3, 1)).reshape(N, KV_W, R)


def _prompt_layer(x, mod, prm, tm, tq):
    B, T, _ = x.shape
    mod = mod[:, None, :]
    f = _front(x, mod, jnp.arange(T, dtype=jnp.int32), prm, tm)
    u_out, conv_state = _conv(f["u"], jnp.zeros((B, CONV_K - 1, D_CONV), F32), prm, tm)
    kcT, vcT = _compress(f["kcmpT"], f["vcmpT"], prm, 2048)
    o = _nsa_prompt(f, jnp.swapaxes(kcT, 1, 2).astype(BF16), vcT.astype(BF16), tq)
    y = _back(x, o, u_out, f["gmg"], mod, prm, tm)
    keep = min(WINDOW, T)
    rows = _rows_from_channel_major
    return y, (rows(f["kcmpT"]), rows(f["vcmpT"]), rows(f["kselT"]), rows(f["vselT"]),
               rows(f["kwinT"][:, :, T - keep:]), rows(f["vwinT"][:, :, T - keep:]), conv_state)


def _sample_layer(x, mod, cache_cmp_k, cache_cmp_v, cache_sel_k, cache_sel_v, cache_win_k, cache_win_v,
                  state_conv, page_table, prm):
    Bs, Ts, _ = x.shape
    n_pages = page_table.shape[1]
    past_len = n_pages * PAGE
    n_tok = Bs * Ts
    tqs = LANES // (N_KV * GROUP)
    assert n_tok % LANES == 0 and Ts <= tqs and n_pages % min(16, n_pages) == 0
    pos = past_len + jnp.arange(n_tok, dtype=jnp.int32) % Ts
    mod_tok = jnp.repeat(mod, Ts, axis=0)[None]
    f = _front(x.reshape(1, n_tok, D_MODEL), mod_tok, pos, prm, n_tok)
    u_out, conv_state = _conv(f["u"].reshape(Bs, Ts, D_CONV), state_conv, prm, Ts)

    def lanes(a, rows, width):
        a = a.reshape(rows + (Bs, Ts))
        a = jnp.moveaxis(a, len(rows), 0)
        return jnp.pad(a, ((0, 0),) * (len(rows) + 1) + ((0, width - Ts),))

    kcT_past, vcT_past = _compress_paged(_channel_major(cache_cmp_k), _channel_major(cache_cmp_v), page_table, prm,
                                         min(16, n_pages))
    kcT_new, vcT_new = _compress(lanes(f["kcmpT"][0], (KV_W,), BLOCK), lanes(f["vcmpT"][0], (KV_W,), BLOCK), prm, BLOCK)
    nb_past = past_len // BLOCK
    cat = lambda a, b: jnp.concatenate([a[:, :, :nb_past], b], axis=2)
    kcb = jnp.swapaxes(cat(kcT_past, kcT_new), 1, 2).astype(BF16)
    vcTb = cat(vcT_past, vcT_new).astype(BF16)

    def q_operand(qT):
        q = lanes(qT[0], (N_KV, GROUP, HEAD_DIM), tqs)
        q = jnp.moveaxis(q, 2, 3).reshape(Bs, N_KV, HEAD_DIM, GROUP * tqs)
        z = jnp.zeros_like(q[:, 0])
        return jnp.concatenate([jnp.concatenate([q[:, 0], z], axis=2), jnp.concatenate([z, q[:, 1]], axis=2)], axis=1)

    g = lanes(f["ngT"][0, :3 * N_HEADS], (N_KV, GROUP, 3), tqs)
    g = jnp.transpose(g, (0, 3, 1, 2, 4)).reshape(Bs, 3, N_KV * GROUP * tqs)
    gates = jnp.pad(g, ((0, 0), (0, 5), (0, 0)))
    rows_nat = lambda a: jnp.pad(a[..., :KV_W].reshape(Bs, Ts, KV_W), ((0, 0), (0, tqs - Ts), (0, 0)))
    rows_T = lambda a: lanes(a[0].astype(BF16), (KV_W,), tqs)
    oT = _nsa_sample(q_operand(f["qp"]), q_operand(f["qr"]), kcb, vcTb, gates,
                     _channel_major(cache_sel_k), _channel_major(cache_sel_v), page_table,
                     rows_nat(f["kselb"]), rows_T(f["vselT"]),
                     _channel_major(cache_win_k), _channel_major(cache_win_v),
                     rows_nat(f["kwinb"]), rows_T(f["vwinT"]), tqs, min(16, n_pages))
    o = oT.reshape(Bs, HEAD_DIM, N_KV, GROUP, tqs)[..., :Ts]
    o = jnp.transpose(o, (0, 4, 2, 3, 1)).reshape(1, n_tok, Q_W).astype(BF16)
    y = _back(x.reshape(1, n_tok, D_MODEL), o, u_out.reshape(1, n_tok, D_MODEL), f["gmg"], mod_tok, prm, n_tok)
    hs = lambda a: a[0].T.reshape(Bs, Ts, N_KV, HEAD_DIM)
    buf = cache_win_k.shape[1]
    keep = min(WINDOW, buf + Ts)
    kw = jnp.concatenate([cache_win_k, hs(f["kwinT"])], axis=1)[:, -keep:]
    vw = jnp.concatenate([cache_win_v, hs(f["vwinT"])], axis=1)[:, -keep:]
    return y.reshape(Bs, Ts, D_MODEL), (hs(f["kcmpT"]), hs(f["vcmpT"]), hs(f["kselT"]), hs(f["vselT"]), kw, vw, conv_state)


def kernel(x_prompt, x_sample, cache_cmp_k, cache_cmp_v, cache_sel_k, cache_sel_v, cache_win_k, cache_win_v,
           state_conv, page_table, c_prompt, c_sample, w_ada, b_ada, mix_norm_g, w_in, q_norm_g, k_norm_g,
           w_dw, b_dw, conv_ln_g, conv_ln_b, w_pw2, cmp_mod_k, cmp_w_k, cmp_mod_v, cmp_w_v, w_nsa_o, w_out,
           ffn_norm_g, w_gate, w_up, w_down):
    depth = w_in.shape[0]
    yp, ys = x_prompt, x_sample
    st_p, st_s = [], []
    for l in range(depth):
        p = dict(w_ada=w_ada[l], b_ada=b_ada[l], mix_norm_g=mix_norm_g[l], w_in=w_in[l],
                 q_norm_g=q_norm_g[l], k_norm_g=k_norm_g[l], w_dw=w_dw[l], b_dw=b_dw[l],
                 conv_ln_g=conv_ln_g[l], conv_ln_b=conv_ln_b[l], w_pw2=w_pw2[l],
                 cmp_mod_k=cmp_mod_k[l], cmp_w_k=cmp_w_k[l], cmp_mod_v=cmp_mod_v[l],
                 cmp_w_v=cmp_w_v[l], w_nsa_o=w_nsa_o[l], w_out=w_out[l],
                 ffn_norm_g=ffn_norm_g[l], w_gate=w_gate[l], w_up=w_up[l], w_down=w_down[l])
        prm = _prep_params(p)
        nb_p = c_prompt.shape[0]
        mod = _ada(jnp.concatenate([c_prompt, c_sample], axis=0), p["w_ada"], p["b_ada"])
        yp, sp = _prompt_layer(yp, mod[:nb_p], prm, 256, 128)
        ys, ss = _sample_layer(ys, mod[nb_p:], cache_cmp_k[l], cache_cmp_v[l], cache_sel_k[l], cache_sel_v[l],
                               cache_win_k[l], cache_win_v[l], state_conv[l], page_table, prm)
        st_p.append(sp)
        st_s.append(ss)
    stk = lambda lst, j: jnp.stack([s[j] for s in lst])
    return (yp, ys) + tuple(stk(st_p, j) for j in range(7)) + tuple(stk(st_s, j) for j in range(7))
```

```python
import functools
import math

import jax
import jax.numpy as jnp
from jax import lax
from jax.experimental import pallas as pl
from jax.experimental.pallas import tpu as pltpu

F32 = jnp.float32
BF16 = jnp.bfloat16

D_MODEL = 1024
D_CONV = 512
CONV_K = 31
N_HEADS = 8
N_KV = 2
HEAD_DIM = 64
GROUP = 4
Q_W = N_HEADS * HEAD_DIM
KV_W = N_KV * HEAD_DIM
BLOCK = 64
N_SEL = 16
WINDOW = 512
ROPE_THETA = 10000.0
D_FF = 2816
EPS = 1e-6
FORCED_BONUS = 2.0 * GROUP
PAGE = 128
OFF_Q = 2 * D_CONV
OFF_KV = OFF_Q + Q_W
OFF_NG = OFF_KV + 6 * KV_W
OFF_MG = OFF_NG + 3 * N_HEADS
Q_SCALE = HEAD_DIM ** -0.5 * math.log2(math.e)
NEG = -(2.0 ** 100)
M_INIT = -(2.0 ** 99)
ONES_ROWS = 16
V_AUG = HEAD_DIM + ONES_ROWS
K_AUG = KV_W + 16
N_T_ROWS = 1312
CONV_HALO = 32
SUBLANES = 8
LANES = 128
SEL_CHUNK = 512
VMEM_LIMIT = 56 * 1024 * 1024


def _cparams(sem):
    return pltpu.CompilerParams(dimension_semantics=sem, vmem_limit_bytes=VMEM_LIMIT)


def _ada_kernel(c_ref, w_ref, b_ref, o_ref):
    c = c_ref[...]
    s = c * jax.nn.sigmoid(c)
    o_ref[...] = jnp.dot(s.astype(BF16), w_ref[...].astype(BF16), preferred_element_type=F32) + b_ref[...]


def _ada(c, w_ada, b_ada):
    n, tn = c.shape[0], 1024
    return pl.pallas_call(
        _ada_kernel,
        grid=(w_ada.shape[1] // tn,),
        in_specs=[pl.BlockSpec((n, D_MODEL), lambda j: (0, 0)),
                  pl.BlockSpec((D_MODEL, tn), lambda j: (0, j)),
                  pl.BlockSpec((1, tn), lambda j: (0, j))],
        out_specs=pl.BlockSpec((n, tn), lambda j: (0, j)),
        out_shape=jax.ShapeDtypeStruct((n, w_ada.shape[1]), F32),
        compiler_params=_cparams(("arbitrary",)),
        name="ada",
    )(c, w_ada, b_ada.reshape(1, -1))


def _front_kernel(x_ref, sh1_ref, sc1_ref, g_ref, wn_ref, wt_ref, gq_ref, gk_ref, cos_ref, sin_ref,
                  u_ref, gmg_ref, qp_ref, qr_ref, kcmpT_ref, vcmpT_ref, kselT_ref, vselT_ref, kwinT_ref, vwinT_ref,
                  kselb_ref, kwinb_ref, vselTb_ref, vwinTb_ref, ngT_ref):
    x = x_ref[0]
    ms = jnp.mean(x * x, axis=-1, keepdims=True)
    h = x * lax.rsqrt(ms + EPS) * g_ref[...]
    h = h * (1.0 + sc1_ref[0]) + sh1_ref[0]
    hb = h.astype(BF16)
    zn = jnp.dot(hb, wn_ref[...], preferred_element_type=F32)
    u_ref[0] = zn[:, :D_CONV] * jax.nn.sigmoid(zn[:, D_CONV:2 * D_CONV])
    gmg_ref[0] = jax.nn.sigmoid(zn[:, 2 * D_CONV:])
    zt = lax.dot_general(wt_ref[...], hb, (((1,), (1,)), ((), ())), preferred_element_type=F32)
    cos = cos_ref[...]
    sin = sin_ref[...]
    half = HEAD_DIM // 2

    def norm(v, g):
        m = jnp.mean(v * v, axis=0, keepdims=True)
        return v * lax.rsqrt(m + EPS) * g

    def rope(v):
        a, b = v[:half], v[half:]
        return jnp.concatenate([a * cos - b * sin, b * cos + a * sin], axis=0)

    gq = gq_ref[...]
    for hh in range(N_HEADS):
        q = norm(zt[hh * HEAD_DIM:(hh + 1) * HEAD_DIM], gq) * Q_SCALE
        qp_ref[0, hh * HEAD_DIM:(hh + 1) * HEAD_DIM, :] = q.astype(BF16)
        qr_ref[0, hh * HEAD_DIM:(hh + 1) * HEAD_DIM, :] = rope(q).astype(BF16)

    def head_pair(j, g=None, rotary=False):
        outs = []
        for hh in range(N_KV):
            lo = Q_W + j * KV_W + hh * HEAD_DIM
            v = zt[lo:lo + HEAD_DIM]
            if g is not None:
                v = norm(v, g)
            if rotary:
                v = rope(v)
            outs.append(v)
        return jnp.concatenate(outs, axis=0)

    k_sel = head_pair(2, gk_ref[1], True)
    v_sel = head_pair(3)
    k_win = head_pair(4, gk_ref[2], True)
    v_win = head_pair(5)
    kcmpT_ref[0] = head_pair(0, gk_ref[0])
    vcmpT_ref[0] = head_pair(1)
    kselT_ref[0] = k_sel
    vselT_ref[0] = v_sel
    kwinT_ref[0] = k_win
    vwinT_ref[0] = v_win
    tm = x.shape[0]
    blk = lax.shift_right_logical(pl.program_id(1) * tm + lax.broadcasted_iota(jnp.int32, (tm, K_AUG - KV_W), 0), 6)
    onehot = (blk % (SEL_CHUNK // BLOCK)) == lax.broadcasted_iota(jnp.int32, (tm, K_AUG - KV_W), 1)
    kselb_ref[0] = jnp.concatenate([k_sel.T, jnp.where(onehot, 1.0, 0.0)], axis=1).astype(BF16)
    kwinb_ref[0] = k_win.T.astype(BF16)
    ones = jnp.ones((ONES_ROWS, tm), F32)

    def v_aug(v):
        return jnp.concatenate([v[:HEAD_DIM], ones, v[HEAD_DIM:], ones], axis=0).astype(BF16)

    vselTb_ref[0] = v_aug(v_sel)
    vwinTb_ref[0] = v_aug(v_win)
    ngT_ref[0] = jax.nn.sigmoid(zt[Q_W + 6 * KV_W:])


def _front(x, mod, pos, prm, tm):
    B, T, _ = x.shape
    Tm = mod.shape[1]
    tmm = 1 if Tm == 1 else tm
    half = HEAD_DIM // 2
    inv = jnp.power(ROPE_THETA, -jnp.arange(half, dtype=F32) / half)
    ang = pos.astype(F32)[:, None] * inv[None, :]
    cosT, sinT = jnp.cos(ang).T, jnp.sin(ang).T
    gq = jnp.broadcast_to(prm["q_norm_g"][:, None], (HEAD_DIM, tm))
    gk = jnp.broadcast_to(prm["k_norm_g"][:, :, None], (3, HEAD_DIM, tm))
    nt = T // tm
    tok = lambda w: pl.BlockSpec((1, tm, w), lambda b, t: (b, t, 0))
    tokT = lambda r: pl.BlockSpec((1, r, tm), lambda b, t: (b, 0, t))
    modspec = lambda k: pl.BlockSpec((1, tmm, D_MODEL), (lambda b, t: (b, t, k)) if Tm != 1 else (lambda b, t: (b, 0, k)))
    const = lambda shape: pl.BlockSpec(shape, lambda b, t: (0,) * len(shape))
    sd = jax.ShapeDtypeStruct
    outs = pl.pallas_call(
        _front_kernel,
        grid=(B, nt),
        in_specs=[tok(D_MODEL), modspec(0), modspec(1), const((1, D_MODEL)),
                  const((D_MODEL, 3 * D_MODEL)), const((N_T_ROWS, D_MODEL)),
                  const((HEAD_DIM, tm)), const((3, HEAD_DIM, tm)),
                  pl.BlockSpec((half, tm), lambda b, t: (0, t)), pl.BlockSpec((half, tm), lambda b, t: (0, t))],
        out_specs=[tok(D_CONV), tok(2 * D_MODEL), tokT(Q_W), tokT(Q_W)] + [tokT(KV_W)] * 6
                  + [tok(K_AUG), tok(KV_W), tokT(N_KV * V_AUG), tokT(N_KV * V_AUG), tokT(32)],
        out_shape=[sd((B, T, D_CONV), F32), sd((B, T, 2 * D_MODEL), F32), sd((B, Q_W, T), BF16), sd((B, Q_W, T), BF16)]
                  + [sd((B, KV_W, T), F32)] * 6
                  + [sd((B, T, K_AUG), BF16), sd((B, T, KV_W), BF16), sd((B, N_KV * V_AUG, T), BF16),
                     sd((B, N_KV * V_AUG, T), BF16), sd((B, 32, T), F32)],
        compiler_params=_cparams(("parallel", "arbitrary")),
        name="front",
    )(x, mod, mod, prm["mix_norm_g"].reshape(1, -1), prm["wn"], prm["wt"], gq, gk, cosT, sinT)
    names = ("u", "gmg", "qp", "qr", "kcmpT", "vcmpT", "kselT", "vselT", "kwinT", "vwinT",
             "kselb", "kwinb", "vselTb", "vwinTb", "ngT")
    return dict(zip(names, outs))


def _conv_kernel(u_ref, st_ref, wdw_ref, bdw_ref, lng_ref, lnb_ref, wpw_ref, uo_ref, so_ref, ext_ref, sh_ref, *, tm):
    t = pl.program_id(1)
    pad = CONV_HALO - (CONV_K - 1)
    n_sh = tm + CONV_HALO - SUBLANES

    @pl.when(t == 0)
    def _():
        ext_ref[0:pad, :] = jnp.zeros((pad, D_CONV), F32)
        ext_ref[pad:CONV_HALO, :] = st_ref[0]

    ext_ref[CONV_HALO:CONV_HALO + tm, :] = u_ref[0]
    for r in range(1, SUBLANES):
        sh_ref[r - 1] = ext_ref[r:r + n_sh, :]
    y = jnp.zeros((tm, D_CONV), F32) + bdw_ref[...]
    for k in range(CONV_K):
        a, r = divmod(pad + k, SUBLANES)
        rows = ext_ref[SUBLANES * a:SUBLANES * a + tm, :] if r == 0 else sh_ref[r - 1, SUBLANES * a:SUBLANES * a + tm, :]
        y = y + rows * wdw_ref[k:k + 1, :]
    mu = jnp.mean(y, axis=-1, keepdims=True)
    yc = y - mu
    var = jnp.mean(yc * yc, axis=-1, keepdims=True)
    yn = yc * lax.rsqrt(var + EPS) * lng_ref[...] + lnb_ref[...]
    ya = yn * jax.nn.sigmoid(yn)
    uo_ref[0] = jnp.dot(ya.astype(BF16), wpw_ref[...], preferred_element_type=F32)
    so_ref[0] = ext_ref[tm + pad:tm + CONV_HALO, :]
    carry = ext_ref[tm:tm + CONV_HALO, :]
    ext_ref[0:CONV_HALO, :] = carry


def _conv(u, state, prm, tm):
    B, T, _ = u.shape
    const = lambda shape: pl.BlockSpec(shape, lambda b, t: (0,) * len(shape))
    return pl.pallas_call(
        functools.partial(_conv_kernel, tm=tm),
        grid=(B, T // tm),
        in_specs=[pl.BlockSpec((1, tm, D_CONV), lambda b, t: (b, t, 0)),
                  pl.BlockSpec((1, CONV_K - 1, D_CONV), lambda b, t: (b, 0, 0)),
                  const((CONV_K, D_CONV)), const((1, D_CONV)), const((1, D_CONV)), const((1, D_CONV)),
                  const((D_CONV, D_MODEL))],
        out_specs=[pl.BlockSpec((1, tm, D_MODEL), lambda b, t: (b, t, 0)),
                   pl.BlockSpec((1, CONV_K - 1, D_CONV), lambda b, t: (b, 0, 0))],
        out_shape=[jax.ShapeDtypeStruct((B, T, D_MODEL), F32), jax.ShapeDtypeStruct((B, CONV_K - 1, D_CONV), F32)],
        scratch_shapes=[pltpu.VMEM((CONV_HALO + tm, D_CONV), F32),
                        pltpu.VMEM((SUBLANES - 1, tm + CONV_HALO - SUBLANES, D_CONV), F32)],
        compiler_params=_cparams(("parallel", "arbitrary")),
        name="conv",
    )(u, state, prm["w_dw"], prm["b_dw"].reshape(1, -1), prm["conv_ln_g"].reshape(1, -1),
      prm["conv_ln_b"].reshape(1, -1), prm["w_pw2b"])


GATHER_SLOTS = 3


def _gather_pages(pt_ref, srcs, buf_ref, sem_ref, pages_per_step):
    step = pl.program_id(0) * pl.num_programs(1) + pl.program_id(1)
    n_steps = pl.num_programs(0) * pl.num_programs(1)
    ahead = GATHER_SLOTS - 1

    def copies(s):
        slot = s % GATHER_SLOTS
        return [pltpu.make_async_copy(src.at[pt_ref[s * pages_per_step + i]], buf_ref.at[slot, a, i], sem_ref.at[slot])
                for a, src in enumerate(srcs) for i in range(pages_per_step)]

    def start(s):
        for n, c in enumerate(copies(s)):
            c.start(priority=n % 2)

    for d in range(ahead):

        @pl.when((step == 0) & (d < n_steps))
        def _():
            start(d)

    @pl.when(step + ahead < n_steps)
    def _():
        start(step + ahead)

    for c in copies(step):
        c.wait()
    slot = step % GATHER_SLOTS
    return [jnp.concatenate([buf_ref[slot, a, i] for i in range(pages_per_step)], axis=1) for a in range(len(srcs))]


def _compress_core(xk, xv, modk_ref, wk_ref, modv_ref, wv_ref, s_ref, kc_ref, vc_ref, acc_ref, group):
    slot = pl.program_id(1) % group

    @pl.when(slot == 0)
    def _():
        acc_ref[...] = jnp.zeros_like(acc_ref)

    place = s_ref[slot]
    rows = [(x * (1.0 + mod_ref[...])).astype(BF16) for x, mod_ref in ((xk, modk_ref), (xv, modv_ref))]
    sums = jnp.dot(jnp.concatenate(rows, axis=0), place, preferred_element_type=F32)
    for i, (w_ref, out_ref) in enumerate(((wk_ref, kc_ref), (wv_ref, vc_ref))):
        acc_ref[i] += sums[i * KV_W:(i + 1) * KV_W]
        out_ref[0] = jnp.dot(w_ref[...], acc_ref[i].astype(BF16), preferred_element_type=F32)


def _compress_kernel(k_ref, v_ref, *refs, group):
    _compress_core(k_ref[0], v_ref[0], *refs, group=group)


def _compress_paged_kernel(pt_ref, k_hbm, v_hbm, *refs, group, pages_per_step):
    *core, buf_ref, sem_ref = refs
    xk, xv = _gather_pages(pt_ref, (k_hbm, v_hbm), buf_ref, sem_ref, pages_per_step)
    _compress_core(xk, xv, *core, group=group)


def _compress_consts(prm, cols):
    nbs = max(cols // BLOCK, 1)
    group = LANES // nbs
    modT = lambda m: jnp.tile(m.T, (N_KV, max(cols // BLOCK, 1)))[:, :cols]
    wT = lambda w: jnp.kron(jnp.eye(N_KV, dtype=F32), w.T).astype(BF16)
    r = jnp.arange(cols)[None, :, None]
    g = jnp.arange(group)[:, None, None]
    lane = jnp.arange(LANES)[None, None, :]
    place = jnp.where(lane == g * nbs + r // BLOCK, 1.0 / BLOCK, 0.0).astype(BF16)
    return (modT(prm["cmp_mod_k"]), wT(prm["cmp_w_k"]), modT(prm["cmp_mod_v"]), wT(prm["cmp_w_v"]), place), group


def _compress_specs(cols, group, n_batch, n_out_blocks):
    const = lambda shape: pl.BlockSpec(shape, lambda *a: (0,) * len(shape))
    ospec = pl.BlockSpec((1, KV_W, LANES), lambda b, j, *a: (b, 0, j // group))
    osh = jax.ShapeDtypeStruct((n_batch, KV_W, n_out_blocks * LANES), F32)
    const_specs = [const((KV_W, cols)), const((KV_W, KV_W)), const((KV_W, cols)), const((KV_W, KV_W)),
                   const((group, cols, LANES))]
    return const_specs, [ospec, ospec], [osh, osh], [pltpu.VMEM((2, KV_W, LANES), F32)]


def _compress(kT, vT, prm, cols):
    B, _, L = kT.shape
    cols = min(cols, L)
    n_blocks = -(-(L // BLOCK) // LANES) if L >= BLOCK else 1
    consts, group = _compress_consts(prm, cols)
    const_specs, out_specs, out_shape, scratch = _compress_specs(cols, group, B, n_blocks)
    x_spec = pl.BlockSpec((1, KV_W, cols), lambda b, j: (b, 0, j))
    return pl.pallas_call(
        functools.partial(_compress_kernel, group=group), grid=(B, L // cols),
        in_specs=[x_spec, x_spec] + const_specs, out_specs=out_specs, out_shape=out_shape, scratch_shapes=scratch,
        compiler_params=_cparams(("parallel", "arbitrary")), name="compress",
    )(kT, vT, *consts)


def _compress_paged(cache_kT, cache_vT, page_table, prm, pages_per_step):
    B, n_pages = page_table.shape
    cols = pages_per_step * PAGE
    n_blocks = -(-(n_pages * PAGE // BLOCK) // LANES)
    consts, group = _compress_consts(prm, cols)
    const_specs, out_specs, out_shape, scratch = _compress_specs(cols, group, B, n_blocks)
    hbm = pl.BlockSpec(memory_space=pl.ANY)
    grid_spec = pltpu.PrefetchScalarGridSpec(
        num_scalar_prefetch=1, grid=(B, n_pages // pages_per_step),
        in_specs=[hbm, hbm] + const_specs, out_specs=out_specs,
        scratch_shapes=scratch + [pltpu.VMEM((GATHER_SLOTS, 2, pages_per_step, KV_W, PAGE), F32),
                                  pltpu.SemaphoreType.DMA((GATHER_SLOTS,))])
    return pl.pallas_call(
        functools.partial(_compress_paged_kernel, group=group, pages_per_step=pages_per_step),
        grid_spec=grid_spec, out_shape=out_shape,
        compiler_params=_cparams(("arbitrary", "arbitrary")), name="compress_paged",
    )(page_table.reshape(-1), cache_kT, cache_vT, *consts)


def _attend_scores(s, vT, mask, carry, s_max=None):
    m, acc = carry
    if mask is not None:
        s = jnp.where(mask, s, NEG)
    if s_max is None:
        s_max = jnp.max(s, axis=0, keepdims=True)
    m_new = jnp.maximum(m, s_max)
    alpha = jnp.exp2(m - m_new)
    p = jnp.exp2(s - m_new).astype(BF16)
    return m_new, alpha * acc + jnp.dot(vT, p, preferred_element_type=F32)


def _attend(qT, k, vT, mask, carry):
    return _attend_scores(jnp.dot(k, qT, preferred_element_type=F32), vT, mask, carry)


def _attend_init(L, rows=V_AUG):
    return jnp.full((1, L), M_INIT, F32), jnp.zeros((rows, L), F32)


def _attend_out(carry):
    _, acc = carry
    return acc[:HEAD_DIM] / jnp.maximum(acc[HEAD_DIM:HEAD_DIM + 1], 1e-30)


def _with_ones(vT):
    return jnp.concatenate([vT, jnp.ones((ONES_ROWS, vT.shape[1]), BF16)], axis=0)


def _block_mask(sel_rows, kc):
    nb, L = sel_rows.shape
    return jnp.broadcast_to((sel_rows > 0.5)[:, None, :], (nb, BLOCK, L)).reshape(kc, L)


def _cmp_and_select(qpT, kc, vcT, q0, tq):
    return _cmp_finish(jnp.dot(kc, qpT, preferred_element_type=F32), vcT, q0, tq)


def _cmp_finish(s, vcT, q0, tq):
    nb, L = s.shape
    blk = lax.broadcasted_iota(jnp.int32, (nb, L), 0)
    qpos = q0 + lax.broadcasted_iota(jnp.int32, (1, L), 1) % tq
    ok = (blk + 1) * BLOCK <= qpos + 1
    s = jnp.where(ok, s, -jnp.inf)
    m = jnp.max(s, axis=0, keepdims=True)
    m = jnp.where(m > -jnp.inf, m, 0.0)
    e = jnp.where(ok, jnp.exp2(s - m), 0.0)
    p = e / jnp.maximum(jnp.sum(e, axis=0, keepdims=True), 1e-30)
    o_cmp = jnp.dot(vcT, p.astype(BF16), preferred_element_type=F32)
    if tq % LANES == 0:
        w = tq
        imp = p[:, :tq]
        for g in range(1, GROUP):
            imp = imp + p[:, g * tq:(g + 1) * tq]
    else:
        w = L
        seg = GROUP * tq
        in_seg = lax.broadcasted_iota(jnp.int32, (1, L), 1) % seg
        imp = p
        for g in range(1, GROUP):
            rolled = pltpu.roll(p, g * tq, 1)
            if seg != L:
                rolled = jnp.where(in_seg >= g * tq, rolled, pltpu.roll(p, (g * tq - seg) % L, 1))
            imp = imp + rolled
    if w != L:
        blk = lax.broadcasted_iota(jnp.int32, (nb, w), 0)
        qpos = q0 + lax.broadcasted_iota(jnp.int32, (1, w), 1)
    cur = lax.shift_right_logical(qpos, 6)
    visible = blk <= cur
    forced = (blk == 0) | (blk == cur) | (blk == cur - 1)
    score = jnp.where(visible, imp + jnp.where(forced, FORCED_BONUS, 0.0), -1.0)
    ridx = blk.astype(F32)
    sel = jnp.zeros_like(score)
    rem = score
    for _ in range(min(N_SEL, nb)):
        mx = jnp.max(rem, axis=0, keepdims=True)
        first = jnp.min(jnp.where(rem == mx, ridx, float(nb)), axis=0, keepdims=True)
        pick = ridx == first
        rem = jnp.where(pick, -jnp.inf, rem)
        sel = jnp.where(pick, 1.0, sel)
    if w != L:
        sel = jnp.concatenate([sel] * GROUP, axis=1)
    return o_cmp, sel


def _nsa_prompt_kernel(qp_ref, qr_ref, kc_ref, vcT_ref, ksel_ref, vselT_ref, kwin_ref, vwinT_ref, ngT_ref,
                       o_ref, bias_ref, s_ref, smax_ref, sel_ref, win_ref, diag_ref, *, tq):
    t = pl.program_id(1)
    s0 = t * tq
    L = GROUP * tq
    lane = lax.broadcasted_iota(jnp.int32, (1, L), 1)
    qpos = s0 + lane % tq
    ng = ngT_ref[0]
    heads = [slice(h * HEAD_DIM, (h + 1) * HEAD_DIM) for h in range(N_KV)]

    def qpad(ref, h):
        q = jnp.concatenate([ref[0, (h * GROUP + g) * HEAD_DIM:(h * GROUP + g + 1) * HEAD_DIM, :]
                             for g in range(GROUP)], axis=1)
        z = jnp.zeros_like(q)
        return jnp.concatenate([q, z] if h == 0 else [z, q], axis=0)

    qr = [qpad(qr_ref, h) for h in range(N_KV)]
    vrows = [slice(h * V_AUG, (h + 1) * V_AUG) for h in range(N_KV)]
    n_blk = SEL_CHUNK // BLOCK
    n_full = s0 // SEL_CHUNK
    k_diag = pl.multiple_of(n_full * SEL_CHUNK, SEL_CHUNK)
    w0 = pl.multiple_of(jnp.maximum(s0 - WINDOW, 0), tq)

    s_cmp = [jnp.dot(kc_ref[0], qpad(qp_ref, h), preferred_element_type=F32) for h in range(N_KV)]
    for h in range(N_KV):
        win_ref[h] = jnp.dot(kwin_ref[0, pl.ds(w0, WINDOW + tq), :], qr[h], preferred_element_type=F32)
        diag_ref[h] = jnp.dot(ksel_ref[0, pl.ds(k_diag, SEL_CHUNK), pl.ds(0, KV_W)], qr[h], preferred_element_type=F32)

    o_cmp = []
    for h in range(N_KV):
        o, sel = _cmp_finish(s_cmp[h], vcT_ref[0, heads[h], :], s0, tq)
        o_cmp.append(o)
        sel_ref[h] = sel
        bias = jnp.where(sel > 0.5, 0.0, NEG).reshape(-1, n_blk, L)
        bias = jnp.concatenate([bias, jnp.full((1, n_blk, L), NEG, F32)], axis=0)
        bias_ref[h] = jnp.concatenate([bias, jnp.zeros_like(bias)], axis=1).astype(BF16)

    dead = bias_ref.shape[1] - 1

    def scores(c, bias_idx, h):
        k0 = pl.multiple_of(c * SEL_CHUNK, SEL_CHUNK)
        return jnp.dot(ksel_ref[0, pl.ds(k0, SEL_CHUNK), :], jnp.concatenate([qr[h], bias_ref[h, bias_idx]], axis=0),
                       preferred_element_type=F32)

    def issue(c, slot):
        c_eff = jnp.minimum(c, jnp.maximum(n_full - 1, 0))
        for h in range(N_KV):
            s = scores(c_eff, jnp.where(c < n_full, c, dead), h)
            s_ref[slot, h] = s
            smax_ref[slot, h] = jnp.max(s, axis=0, keepdims=True)

    def consume(c, slot, carries):
        k0 = pl.multiple_of(jnp.minimum(c, jnp.maximum(n_full - 1, 0)) * SEL_CHUNK, SEL_CHUNK)
        return tuple(_attend_scores(s_ref[slot, h], vselT_ref[0, vrows[h], pl.ds(k0, SEL_CHUNK)], None, carries[h],
                                    smax_ref[slot, h]) for h in range(N_KV))

    issue(0, 0)

    kpos = w0 + lax.broadcasted_iota(jnp.int32, (WINDOW + tq, L), 0)
    wmask = (kpos <= qpos) & (kpos > qpos - WINDOW)
    o_win = [_attend_out(_attend_scores(win_ref[h], vwinT_ref[0, vrows[h], pl.ds(w0, WINDOW + tq)], wmask, _attend_init(L)))
             for h in range(N_KV)]

    causal = k_diag + lax.broadcasted_iota(jnp.int32, (SEL_CHUNK, L), 0) <= qpos
    r_diag = pl.multiple_of(n_full * n_blk, n_blk)
    carries = tuple(_attend_scores(diag_ref[h], vselT_ref[0, vrows[h], pl.ds(k_diag, SEL_CHUNK)],
                                   _block_mask(sel_ref[h, pl.ds(r_diag, n_blk), :], SEL_CHUNK) & causal, _attend_init(L))
                    for h in range(N_KV))

    def issue_and_consume(c_issue, slot_i, c_cons, slot_c, carries):
        last = jnp.maximum(n_full - 1, 0)
        ki = pl.multiple_of(jnp.minimum(c_issue, last) * SEL_CHUNK, SEL_CHUNK)
        kc = pl.multiple_of(jnp.minimum(c_cons, last) * SEL_CHUNK, SEL_CHUNK)
        bias_idx = jnp.where(c_issue < n_full, c_issue, dead)
        slab = SEL_CHUNK // 2
        m_new = [jnp.maximum(carries[h][0], smax_ref[slot_c, h]) for h in range(N_KV)]
        acc = [jnp.exp2(carries[h][0] - m_new[h]) * carries[h][1] for h in range(N_KV)]
        q_aug = [jnp.concatenate([qr[h], bias_ref[h, bias_idx]], axis=0) for h in range(N_KV)]
        s_max = [None] * N_KV
        for r in range(SEL_CHUNK // slab):
            rows = slice(r * slab, (r + 1) * slab)
            for h in range(N_KV):
                s = jnp.dot(ksel_ref[0, pl.ds(ki + r * slab, slab), :], q_aug[h], preferred_element_type=F32)
                s_ref[slot_i, h, rows, :] = s
                mx = jnp.max(s, axis=0, keepdims=True)
                s_max[h] = mx if s_max[h] is None else jnp.maximum(s_max[h], mx)
                p = jnp.exp2(s_ref[slot_c, h, rows, :] - m_new[h]).astype(BF16)
                acc[h] = acc[h] + jnp.dot(vselT_ref[0, vrows[h], pl.ds(kc + r * slab, slab)], p,
                                          preferred_element_type=F32)
        for h in range(N_KV):
            smax_ref[slot_i, h] = s_max[h]
        return tuple((m_new[h], acc[h]) for h in range(N_KV))

    def pair(i, carries):
        c = 2 * i
        carries = issue_and_consume(c + 1, 1, c, 0, carries)
        return issue_and_consume(c + 2, 0, c + 1, 1, carries)

    carries = lax.fori_loop(0, (n_full + 1) // 2, pair, carries)

    def gate(h, br):
        return jnp.concatenate([ng[h * 12 + g * 3 + br:h * 12 + g * 3 + br + 1, :] for g in range(GROUP)], axis=1)

    for h in range(N_KV):
        o_sel = _attend_out(carries[h])
        o = gate(h, 0) * o_cmp[h] + gate(h, 1) * o_sel + gate(h, 2) * o_win[h]
        oT = jnp.concatenate([o[:, g * tq:(g + 1) * tq] for g in range(GROUP)], axis=0)
        o_ref[0, :, h * GROUP * HEAD_DIM:(h + 1) * GROUP * HEAD_DIM] = oT.T.astype(o_ref.dtype)


def _nsa_prompt(f, kcb, vcTb, tq):
    B, _, T = f["qp"].shape
    assert T % SEL_CHUNK == 0 and SEL_CHUNK % tq == 0 and WINDOW % tq == 0 and T >= WINDOW + tq
    nbp = kcb.shape[1]
    full = lambda shape: pl.BlockSpec((1,) + shape, lambda b, t: (b, 0, 0))
    return pl.pallas_call(
        functools.partial(_nsa_prompt_kernel, tq=tq),
        grid=(B, T // tq),
        in_specs=[pl.BlockSpec((1, Q_W, tq), lambda b, t: (b, 0, t)), pl.BlockSpec((1, Q_W, tq), lambda b, t: (b, 0, t)),
                  full((nbp, KV_W)), full((KV_W, nbp)),
                  full((T, K_AUG)), full((N_KV * V_AUG, T)), full((T, KV_W)), full((N_KV * V_AUG, T)),
                  pl.BlockSpec((1, 32, tq), lambda b, t: (b, 0, t))],
        out_specs=pl.BlockSpec((1, tq, Q_W), lambda b, t: (b, t, 0)),
        out_shape=jax.ShapeDtypeStruct((B, T, Q_W), BF16),
        scratch_shapes=[pltpu.VMEM((N_KV, nbp // (SEL_CHUNK // BLOCK) + 1, 2 * (SEL_CHUNK // BLOCK), GROUP * tq), BF16),
                        pltpu.VMEM((2, N_KV, SEL_CHUNK, GROUP * tq), F32), pltpu.VMEM((2, N_KV, 1, GROUP * tq), F32),
                        pltpu.VMEM((N_KV, nbp, GROUP * tq), F32),
                        pltpu.VMEM((N_KV, WINDOW + tq, GROUP * tq), F32),
                        pltpu.VMEM((N_KV, SEL_CHUNK, GROUP * tq), F32)],
        compiler_params=_cparams(("parallel", "arbitrary")),
        name="nsa_prompt",
    )(f["qp"], f["qr"], kcb, vcTb, f["kselb"], f["vselTb"], f["kwinb"], f["vwinTb"], f["ngT"])


def _two_head_values(vT):
    ones = jnp.ones((ONES_ROWS, vT.shape[1]), BF16)
    return jnp.concatenate([vT[:HEAD_DIM], ones, vT[HEAD_DIM:], ones], axis=0)


def _own_head_rows(x, lane_head):
    r = x.shape[0] // N_KV
    return jnp.where(lane_head == 0, x[:r], x[r:])


def _nsa_sample_kernel(pt_ref, qp_ref, qr_ref, kc_ref, vcT_ref, gate_ref, k_hbm, v_hbm,
                       knew_ref, vnewT_ref, cwkT_ref, cwvT_ref, kwnew_ref, vwnewT_ref,
                       o_ref, sel_ref, ocmp_ref, m_ref, acc_ref, buf_ref, sem_ref, *, tqs, past_len, pages_per_step):
    kT_pages, vT_pages = _gather_pages(pt_ref, (k_hbm, v_hbm), buf_ref, sem_ref, pages_per_step)
    j = pl.program_id(1)
    n_steps = pl.num_programs(1)
    L = N_KV * GROUP * tqs
    kc_step = pages_per_step * PAGE
    lane = lax.broadcasted_iota(jnp.int32, (1, L), 1)
    qi = lane % tqs
    lane_head = lane // (GROUP * tqs)

    def out(carry):
        acc = _own_head_rows(carry[1], lane_head)
        return acc[:HEAD_DIM] / jnp.maximum(acc[HEAD_DIM:HEAD_DIM + 1], 1e-30)

    @pl.when(j == 0)
    def _():
        o_cmp, sel = _cmp_and_select(qp_ref[0], kc_ref[0], vcT_ref[0], past_len, tqs)
        sel_ref[...] = sel
        ocmp_ref[...] = _own_head_rows(o_cmp, lane_head)
        m_ref[...], acc_ref[...] = _attend_init(L, N_KV * V_AUG)

    k = kT_pages.T.astype(BF16)
    r0 = pl.multiple_of(j * (kc_step // BLOCK), kc_step // BLOCK)
    mask = _block_mask(sel_ref[pl.ds(r0, kc_step // BLOCK), :], kc_step)
    vT = _two_head_values(vT_pages.astype(BF16))
    half = kc_step // 2
    scores = [jnp.dot(k[i * half:(i + 1) * half], qr_ref[0], preferred_element_type=F32) for i in range(2)]
    carry = (m_ref[...], acc_ref[...])
    for i in range(2):
        carry = _attend_scores(scores[i], vT[:, i * half:(i + 1) * half], mask[i * half:(i + 1) * half], carry)
    m_ref[...], acc_ref[...] = carry

    @pl.when(j == n_steps - 1)
    def _():
        nb_past = past_len // BLOCK
        buf = cwkT_ref.shape[2]
        r_new = lax.broadcasted_iota(jnp.int32, (tqs, L), 0)
        r_buf = lax.broadcasted_iota(jnp.int32, (buf, L), 0)
        qr = qr_ref[0]
        mask_new = (sel_ref[nb_past:nb_past + 1, :] > 0.5) & (r_new <= qi)
        o_sel = out(_attend(qr, knew_ref[0], _two_head_values(vnewT_ref[0]), mask_new, (m_ref[...], acc_ref[...])))
        carry = _attend(qr, cwkT_ref[0].T.astype(BF16), _two_head_values(cwvT_ref[0].astype(BF16)),
                        r_buf > qi + (buf - WINDOW), _attend_init(L, N_KV * V_AUG))
        o_win = out(_attend(qr, kwnew_ref[0], _two_head_values(vwnewT_ref[0]), r_new <= qi, carry))
        g = gate_ref[0]
        o_ref[0] = g[0:1] * ocmp_ref[...] + g[1:2] * o_sel + g[2:3] * o_win


def _nsa_sample(qpT, qrT, kcb, vcTb, gates, cache_sel_kT, cache_sel_vT, page_table, knew, vnewT,
                cache_win_kT, cache_win_vT, kwnew, vwnewT, tqs, pages_per_step):
    Bs, n_pages = page_table.shape
    L = N_KV * GROUP * tqs
    nbp = kcb.shape[1]
    buf = cache_win_kT.shape[2]
    per_b = lambda shape: pl.BlockSpec((1,) + shape, lambda b, j, pt: (b,) + (0,) * len(shape))
    hbm = pl.BlockSpec(memory_space=pl.ANY)
    grid_spec = pltpu.PrefetchScalarGridSpec(
        num_scalar_prefetch=1,
        grid=(Bs, n_pages // pages_per_step),
        in_specs=[per_b((KV_W, L)), per_b((KV_W, L)), per_b((nbp, KV_W)), per_b((KV_W, nbp)), per_b((8, L))]
                 + [hbm, hbm]
                 + [per_b((tqs, KV_W)), per_b((KV_W, tqs)), per_b((KV_W, buf)), per_b((KV_W, buf)),
                    per_b((tqs, KV_W)), per_b((KV_W, tqs))],
        out_specs=per_b((HEAD_DIM, L)),
        scratch_shapes=[pltpu.VMEM((nbp, L), F32), pltpu.VMEM((HEAD_DIM, L), F32),
                        pltpu.VMEM((1, L), F32), pltpu.VMEM((N_KV * V_AUG, L), F32),
                        pltpu.VMEM((GATHER_SLOTS, 2, pages_per_step, KV_W, PAGE), F32),
                        pltpu.SemaphoreType.DMA((GATHER_SLOTS,))],
    )
    return pl.pallas_call(
        functools.partial(_nsa_sample_kernel, tqs=tqs, past_len=n_pages * PAGE, pages_per_step=pages_per_step),
        grid_spec=grid_spec,
        out_shape=jax.ShapeDtypeStruct((Bs, HEAD_DIM, L), F32),
        compiler_params=_cparams(("arbitrary", "arbitrary")),
        name="nsa_sample",
    )(page_table.reshape(-1), qpT, qrT, kcb, vcTb, gates, cache_sel_kT, cache_sel_vT,
      knew, vnewT, cache_win_kT, cache_win_vT, kwnew, vwnewT)


def _back_kernel(x_ref, o_ref, uo_ref, gmg_ref, g1_ref, sh2_ref, sc2_ref, g2_ref, fg_ref,
                 wno_ref, wout_ref, wg_ref, wu_ref, wd_ref, y_ref):
    nsa = jnp.dot(o_ref[0], wno_ref[...], preferred_element_type=F32)
    gmg = gmg_ref[0]
    merged = gmg[:, :D_MODEL] * uo_ref[0] + gmg[:, D_MODEL:] * nsa
    mix = jnp.dot(merged.astype(BF16), wout_ref[...], preferred_element_type=F32)
    x1 = x_ref[0] + g1_ref[0] * mix
    ms = jnp.mean(x1 * x1, axis=-1, keepdims=True)
    h = x1 * lax.rsqrt(ms + EPS) * fg_ref[...]
    h = h * (1.0 + sc2_ref[0]) + sh2_ref[0]
    hb = h.astype(BF16)
    a = jnp.dot(hb, wg_ref[...], preferred_element_type=F32)
    b = jnp.dot(hb, wu_ref[...], preferred_element_type=F32)
    act = a * jax.nn.sigmoid(a) * b
    ffn = jnp.dot(act.astype(BF16), wd_ref[...], preferred_element_type=F32)
    y_ref[0] = x1 + g2_ref[0] * ffn


def _back(x, o, u_out, gmg, mod, prm, tm):
    B, T, _ = x.shape
    Tm = mod.shape[1]
    tmm = 1 if Tm == 1 else tm
    tok = lambda w: pl.BlockSpec((1, tm, w), lambda b, t: (b, t, 0))
    modspec = lambda k: pl.BlockSpec((1, tmm, D_MODEL), (lambda b, t: (b, t, k)) if Tm != 1 else (lambda b, t: (b, 0, k)))
    wspec = lambda shape: pl.BlockSpec(shape, lambda b, t: (0, 0), pipeline_mode=pl.Buffered(1))
    return pl.pallas_call(
        _back_kernel,
        grid=(B, T // tm),
        in_specs=[tok(D_MODEL), tok(Q_W), tok(D_MODEL), tok(2 * D_MODEL),
                  modspec(2), modspec(3), modspec(4), modspec(5), wspec((1, D_MODEL)),
                  wspec((Q_W, D_MODEL)), wspec((D_MODEL, D_MODEL)), wspec((D_MODEL, D_FF)), wspec((D_MODEL, D_FF)),
                  wspec((D_FF, D_MODEL))],
        out_specs=tok(D_MODEL),
        out_shape=jax.ShapeDtypeStruct((B, T, D_MODEL), F32),
        compiler_params=_cparams(("parallel", "arbitrary")),
        name="back",
    )(x, o, u_out, gmg, mod, mod, mod, mod, prm["ffn_norm_g"].reshape(1, -1),
      prm["w_nsa_ob"], prm["w_outb"], prm["w_gateb"], prm["w_upb"], prm["w_downb"])


def _prep_params(p):
    w_in = p["w_in"]
    q = dict(p)
    q["wn"] = jnp.concatenate([w_in[:, :OFF_Q], w_in[:, OFF_MG:]], axis=1).astype(BF16)
    wt = jnp.concatenate([w_in[:, OFF_Q:OFF_MG], jnp.zeros((D_MODEL, N_T_ROWS - (OFF_MG - OFF_Q)), F32)], axis=1)
    q["wt"] = wt.T.astype(BF16)
    q["w_pw2b"] = p["w_pw2"].astype(BF16)
    for name in ("w_nsa_o", "w_out", "w_gate", "w_up", "w_down"):
        q[name + "b"] = p[name].astype(BF16)
    return q


def _rows_from_channel_major(a):
    B, _, T = a.shape
    return jnp.transpose(a.reshape(B, N_KV, HEAD_DIM, T), (0, 3, 1, 2))


def _channel_major(a):
    N, R = a.shape[:2]
    return jnp.transpose(a, (0, 2, 3, 1)).reshape(N, KV_W, R)


def _prompt_layer(x, mod, prm, tm, tq):
    B, T, _ = x.shape
    mod = mod[:, None, :]
    f = _front(x, mod, jnp.arange(T, dtype=jnp.int32), prm, tm)
    u_out, conv_state = _conv(f["u"], jnp.zeros((B, CONV_K - 1, D_CONV), F32), prm, tm)
    kcT, vcT = _compress(f["kcmpT"], f["vcmpT"], prm, 2048)
    o = _nsa_prompt(f, jnp.swapaxes(kcT, 1, 2).astype(BF16), vcT.astype(BF16), tq)
    y = _back(x, o, u_out, f["gmg"], mod, prm, tm)
    keep = min(WINDOW, T)
    rows = _rows_from_channel_major
    return y, (rows(f["kcmpT"]), rows(f["vcmpT"]), rows(f["kselT"]), rows(f["vselT"]),
               rows(f["kwinT"][:, :, T - keep:]), rows(f["vwinT"][:, :, T - keep:]), conv_state)


def _sample_layer(x, mod, cache_cmp_k, cache_cmp_v, cache_sel_k, cache_sel_v, cache_win_k, cache_win_v,
                  state_conv, page_table, prm):
    Bs, Ts, _ = x.shape
    n_pages = page_table.shape[1]
    past_len = n_pages * PAGE
    n_tok = Bs * Ts
    tqs = LANES // (N_KV * GROUP)
    assert n_tok % LANES == 0 and Ts <= tqs and n_pages % min(16, n_pages) == 0
    pos = past_len + jnp.arange(n_tok, dtype=jnp.int32) % Ts
    mod_tok = jnp.repeat(mod, Ts, axis=0)[None]
    f = _front(x.reshape(1, n_tok, D_MODEL), mod_tok, pos, prm, n_tok)
    u_out, conv_state = _conv(f["u"].reshape(Bs, Ts, D_CONV), state_conv, prm, Ts)

    def lanes(a, rows, width):
        a = a.reshape(rows + (Bs, Ts))
        a = jnp.moveaxis(a, len(rows), 0)
        return jnp.pad(a, ((0, 0),) * (len(rows) + 1) + ((0, width - Ts),))

    kcT_past, vcT_past = _compress_paged(_channel_major(cache_cmp_k), _channel_major(cache_cmp_v), page_table, prm,
                                         min(16, n_pages))
    kcT_new, vcT_new = _compress(lanes(f["kcmpT"][0], (KV_W,), BLOCK), lanes(f["vcmpT"][0], (KV_W,), BLOCK), prm, BLOCK)
    nb_past = past_len // BLOCK
    cat = lambda a, b: jnp.concatenate([a[:, :, :nb_past], b], axis=2)
    kcb = jnp.swapaxes(cat(kcT_past, kcT_new), 1, 2).astype(BF16)
    vcTb = cat(vcT_past, vcT_new).astype(BF16)

    def q_operand(qT):
        q = lanes(qT[0], (N_KV, GROUP, HEAD_DIM), tqs)
        q = jnp.moveaxis(q, 2, 3).reshape(Bs, N_KV, HEAD_DIM, GROUP * tqs)
        z = jnp.zeros_like(q[:, 0])
        return jnp.concatenate([jnp.concatenate([q[:, 0], z], axis=2), jnp.concatenate([z, q[:, 1]], axis=2)], axis=1)

    g = lanes(f["ngT"][0, :3 * N_HEADS], (N_KV, GROUP, 3), tqs)
    g = jnp.transpose(g, (0, 3, 1, 2, 4)).reshape(Bs, 3, N_KV * GROUP * tqs)
    gates = jnp.pad(g, ((0, 0), (0, 5), (0, 0)))
    rows_nat = lambda a: jnp.pad(a[..., :KV_W].reshape(Bs, Ts, KV_W), ((0, 0), (0, tqs - Ts), (0, 0)))
    rows_T = lambda a: lanes(a[0].astype(BF16), (KV_W,), tqs)
    oT = _nsa_sample(q_operand(f["qp"]), q_operand(f["qr"]), kcb, vcTb, gates,
                     _channel_major(cache_sel_k), _channel_major(cache_sel_v), page_table,
                     rows_nat(f["kselb"]), rows_T(f["vselT"]),
                     _channel_major(cache_win_k), _channel_major(cache_win_v),
                     rows_nat(f["kwinb"]), rows_T(f["vwinT"]), tqs, min(16, n_pages))
    o = oT.reshape(Bs, HEAD_DIM, N_KV, GROUP, tqs)[..., :Ts]
    o = jnp.transpose(o, (0, 4, 2, 3, 1)).reshape(1, n_tok, Q_W).astype(BF16)
    y = _back(x.reshape(1, n_tok, D_MODEL), o, u_out.reshape(1, n_tok, D_MODEL), f["gmg"], mod_tok, prm, n_tok)
    hs = lambda a: a[0].T.reshape(Bs, Ts, N_KV, HEAD_DIM)
    buf = cache_win_k.shape[1]
    keep = min(WINDOW, buf + Ts)
    kw = jnp.concatenate([cache_win_k, hs(f["kwinT"])], axis=1)[:, -keep:]
    vw = jnp.concatenate([cache_win_v, hs(f["vwinT"])], axis=1)[:, -keep:]
    return y.reshape(Bs, Ts, D_MODEL), (hs(f["kcmpT"]), hs(f["vcmpT"]), hs(f["kselT"]), hs(f["vselT"]), kw, vw, conv_state)


def kernel(x_prompt, x_sample, cache_cmp_k, cache_cmp_v, cache_sel_k, cache_sel_v, cache_win_k, cache_win_v,
           state_conv, page_table, c_prompt, c_sample, w_ada, b_ada, mix_norm_g, w_in, q_norm_g, k_norm_g,
           w_dw, b_dw, conv_ln_g, conv_ln_b, w_pw2, cmp_mod_k, cmp_w_k, cmp_mod_v, cmp_w_v, w_nsa_o, w_out,
           ffn_norm_g, w_gate, w_up, w_down):
    depth = w_in.shape[0]
    yp, ys = x_prompt, x_sample
    st_p, st_s = [], []
    for l in range(depth):
        p = dict(w_ada=w_ada[l], b_ada=b_ada[l], mix_norm_g=mix_norm_g[l], w_in=w_in[l],
                 q_norm_g=q_norm_g[l], k_norm_g=k_norm_g[l], w_dw=w_dw[l], b_dw=b_dw[l],
                 conv_ln_g=conv_ln_g[l], conv_ln_b=conv_ln_b[l], w_pw2=w_pw2[l],
                 cmp_mod_k=cmp_mod_k[l], cmp_w_k=cmp_w_k[l], cmp_mod_v=cmp_mod_v[l],
                 cmp_w_v=cmp_w_v[l], w_nsa_o=w_nsa_o[l], w_out=w_out[l],
                 ffn_norm_g=ffn_norm_g[l], w_gate=w_gate[l], w_up=w_up[l], w_down=w_down[l])
        prm = _prep_params(p)
        nb_p = c_prompt.shape[0]
        mod = _ada(jnp.concatenate([c_prompt, c_sample], axis=0), p["w_ada"], p["b_ada"])
        yp, sp = _prompt_layer(yp, mod[:nb_p], prm, 256, 128)
        ys, ss = _sample_layer(ys, mod[nb_p:], cache_cmp_k[l], cache_cmp_v[l], cache_sel_k[l], cache_sel_v[l],
                               cache_win_k[l], cache_win_v[l], state_conv[l], page_table, prm)
        st_p.append(sp)
        st_s.append(ss)
    stk = lambda lst, j: jnp.stack([s[j] for s in lst])
    return (yp, ys) + tuple(stk(st_p, j) for j in range(7)) + tuple(stk(st_s, j) for j in range(7))
```

```python
import functools
import math

import jax
import jax.numpy as jnp
from jax import lax
from jax.experimental import pallas as pl
from jax.experimental.pallas import tpu as pltpu

F32 = jnp.float32
BF16 = jnp.bfloat16

D_MODEL = 1024
D_CONV = 512
CONV_K = 31
N_HEADS = 8
N_KV = 2
HEAD_DIM = 64
GROUP = 4
Q_W = N_HEADS * HEAD_DIM
KV_W = N_KV * HEAD_DIM
BLOCK = 64
N_SEL = 16
WINDOW = 512
ROPE_THETA = 10000.0
D_FF = 2816
EPS = 1e-6
FORCED_BONUS = 2.0 * GROUP
PAGE = 128
OFF_Q = 2 * D_CONV
OFF_KV = OFF_Q + Q_W
OFF_NG = OFF_KV + 6 * KV_W
OFF_MG = OFF_NG + 3 * N_HEADS
Q_SCALE = HEAD_DIM ** -0.5 * math.log2(math.e)
NEG = -(2.0 ** 100)
M_INIT = -(2.0 ** 99)
ONES_ROWS = 16
V_AUG = HEAD_DIM + ONES_ROWS
K_AUG = KV_W + 16
N_T_ROWS = 1312
CONV_HALO = 32
SUBLANES = 8
LANES = 128
SEL_CHUNK = 512
VMEM_LIMIT = 56 * 1024 * 1024


def _cparams(sem):
    return pltpu.CompilerParams(dimension_semantics=sem, vmem_limit_bytes=VMEM_LIMIT)


def _ada_kernel(c_ref, w_ref, b_ref, o_ref):
    c = c_ref[...]
    s = c * jax.nn.sigmoid(c)
    o_ref[...] = jnp.dot(s.astype(BF16), w_ref[...].astype(BF16), preferred_element_type=F32) + b_ref[...]


def _ada(c, w_ada, b_ada):
    n, tn = c.shape[0], 1024
    return pl.pallas_call(
        _ada_kernel,
        grid=(w_ada.shape[1] // tn,),
        in_specs=[pl.BlockSpec((n, D_MODEL), lambda j: (0, 0)),
                  pl.BlockSpec((D_MODEL, tn), lambda j: (0, j)),
                  pl.BlockSpec((1, tn), lambda j: (0, j))],
        out_specs=pl.BlockSpec((n, tn), lambda j: (0, j)),
        out_shape=jax.ShapeDtypeStruct((n, w_ada.shape[1]), F32),
        compiler_params=_cparams(("arbitrary",)),
        name="ada",
    )(c, w_ada, b_ada.reshape(1, -1))


def _front_kernel(x_ref, sh1_ref, sc1_ref, g_ref, wn_ref, wt_ref, gq_ref, gk_ref, cos_ref, sin_ref,
                  u_ref, gmg_ref, qp_ref, qr_ref, kcmpT_ref, vcmpT_ref, kselT_ref, vselT_ref, kwinT_ref, vwinT_ref,
                  kselb_ref, kwinb_ref, vselTb_ref, vwinTb_ref, ngT_ref):
    x = x_ref[0]
    ms = jnp.mean(x * x, axis=-1, keepdims=True)
    h = x * lax.rsqrt(ms + EPS) * g_ref[...]
    h = h * (1.0 + sc1_ref[0]) + sh1_ref[0]
    hb = h.astype(BF16)
    zn = jnp.dot(hb, wn_ref[...], preferred_element_type=F32)
    u_ref[0] = zn[:, :D_CONV] * jax.nn.sigmoid(zn[:, D_CONV:2 * D_CONV])
    gmg_ref[0] = jax.nn.sigmoid(zn[:, 2 * D_CONV:])
    zt = lax.dot_general(wt_ref[...], hb, (((1,), (1,)), ((), ())), preferred_element_type=F32)
    cos = cos_ref[...]
    sin = sin_ref[...]
    half = HEAD_DIM // 2

    def norm(v, g):
        m = jnp.mean(v * v, axis=0, keepdims=True)
        return v * lax.rsqrt(m + EPS) * g

    def rope(v):
        a, b = v[:half], v[half:]
        return jnp.concatenate([a * cos - b * sin, b * cos + a * sin], axis=0)

    gq = gq_ref[...]
    for hh in range(N_HEADS):
        q = norm(zt[hh * HEAD_DIM:(hh + 1) * HEAD_DIM], gq) * Q_SCALE
        qp_ref[0, hh * HEAD_DIM:(hh + 1) * HEAD_DIM, :] = q.astype(BF16)
        qr_ref[0, hh * HEAD_DIM:(hh + 1) * HEAD_DIM, :] = rope(q).astype(BF16)

    def head_pair(j, g=None, rotary=False):
        outs = []
        for hh in range(N_KV):
            lo = Q_W + j * KV_W + hh * HEAD_DIM
            v = zt[lo:lo + HEAD_DIM]
            if g is not None:
                v = norm(v, g)
            if rotary:
                v = rope(v)
            outs.append(v)
        return jnp.concatenate(outs, axis=0)

    k_sel = head_pair(2, gk_ref[1], True)
    v_sel = head_pair(3)
    k_win = head_pair(4, gk_ref[2], True)
    v_win = head_pair(5)
    kcmpT_ref[0] = head_pair(0, gk_ref[0])
    vcmpT_ref[0] = head_pair(1)
    kselT_ref[0] = k_sel
    vselT_ref[0] = v_sel
    kwinT_ref[0] = k_win
    vwinT_ref[0] = v_win
    tm = x.shape[0]
    blk = lax.shift_right_logical(pl.program_id(1) * tm + lax.broadcasted_iota(jnp.int32, (tm, K_AUG - KV_W), 0), 6)
    onehot = (blk % (SEL_CHUNK // BLOCK)) == lax.broadcasted_iota(jnp.int32, (tm, K_AUG - KV_W), 1)
    kselb_ref[0] = jnp.concatenate([k_sel.T, jnp.where(onehot, 1.0, 0.0)], axis=1).astype(BF16)
    kwinb_ref[0] = k_win.T.astype(BF16)
    ones = jnp.ones((ONES_ROWS, tm), F32)

    def v_aug(v):
        return jnp.concatenate([v[:HEAD_DIM], ones, v[HEAD_DIM:], ones], axis=0).astype(BF16)

    vselTb_ref[0] = v_aug(v_sel)
    vwinTb_ref[0] = v_aug(v_win)
    ngT_ref[0] = jax.nn.sigmoid(zt[Q_W + 6 * KV_W:])


def _front(x, mod, pos, prm, tm):
    B, T, _ = x.shape
    Tm = mod.shape[1]
    tmm = 1 if Tm == 1 else tm
    half = HEAD_DIM // 2
    inv = jnp.power(ROPE_THETA, -jnp.arange(half, dtype=F32) / half)
    ang = pos.astype(F32)[:, None] * inv[None, :]
    cosT, sinT = jnp.cos(ang).T, jnp.sin(ang).T
    gq = jnp.broadcast_to(prm["q_norm_g"][:, None], (HEAD_DIM, tm))
    gk = jnp.broadcast_to(prm["k_norm_g"][:, :, None], (3, HEAD_DIM, tm))
    nt = T // tm
    tok = lambda w: pl.BlockSpec((1, tm, w), lambda b, t: (b, t, 0))
    tokT = lambda r: pl.BlockSpec((1, r, tm), lambda b, t: (b, 0, t))
    modspec = lambda k: pl.BlockSpec((1, tmm, D_MODEL), (lambda b, t: (b, t, k)) if Tm != 1 else (lambda b, t: (b, 0, k)))
    const = lambda shape: pl.BlockSpec(shape, lambda b, t: (0,) * len(shape))
    sd = jax.ShapeDtypeStruct
    outs = pl.pallas_call(
        _front_kernel,
        grid=(B, nt),
        in_specs=[tok(D_MODEL), modspec(0), modspec(1), const((1, D_MODEL)),
                  const((D_MODEL, 3 * D_MODEL)), const((N_T_ROWS, D_MODEL)),
                  const((HEAD_DIM, tm)), const((3, HEAD_DIM, tm)),
                  pl.BlockSpec((half, tm), lambda b, t: (0, t)), pl.BlockSpec((half, tm), lambda b, t: (0, t))],
        out_specs=[tok(D_CONV), tok(2 * D_MODEL), tokT(Q_W), tokT(Q_W)] + [tokT(KV_W)] * 6
                  + [tok(K_AUG), tok(KV_W), tokT(N_KV * V_AUG), tokT(N_KV * V_AUG), tokT(32)],
        out_shape=[sd((B, T, D_CONV), F32), sd((B, T, 2 * D_MODEL), F32), sd((B, Q_W, T), BF16), sd((B, Q_W, T), BF16)]
                  + [sd((B, KV_W, T), F32)] * 6
                  + [sd((B, T, K_AUG), BF16), sd((B, T, KV_W), BF16), sd((B, N_KV * V_AUG, T), BF16),
                     sd((B, N_KV * V_AUG, T), BF16), sd((B, 32, T), F32)],
        compiler_params=_cparams(("parallel", "arbitrary")),
        name="front",
    )(x, mod, mod, prm["mix_norm_g"].reshape(1, -1), prm["wn"], prm["wt"], gq, gk, cosT, sinT)
    names = ("u", "gmg", "qp", "qr", "kcmpT", "vcmpT", "kselT", "vselT", "kwinT", "vwinT",
             "kselb", "kwinb", "vselTb", "vwinTb", "ngT")
    return dict(zip(names, outs))


def _conv_tile(u, st_ref, wdw_ref, bdw_ref, lng_ref, lnb_ref, wpw_ref, uo_ref, so_ref, ext_ref, sh_ref):
    tm = u.shape[0]
    t = pl.program_id(1)
    pad = CONV_HALO - (CONV_K - 1)
    n_sh = tm + CONV_HALO - SUBLANES

    @pl.when(t == 0)
    def _():
        ext_ref[0:pad, :] = jnp.zeros((pad, D_CONV), F32)
        ext_ref[pad:CONV_HALO, :] = st_ref[0]

    ext_ref[CONV_HALO:CONV_HALO + tm, :] = u
    for r in range(1, SUBLANES):
        sh_ref[r - 1] = ext_ref[r:r + n_sh, :]
    y = jnp.zeros((tm, D_CONV), F32) + bdw_ref[...]
    for k in range(CONV_K):
        a, r = divmod(pad + k, SUBLANES)
        rows = ext_ref[SUBLANES * a:SUBLANES * a + tm, :] if r == 0 else sh_ref[r - 1, SUBLANES * a:SUBLANES * a + tm, :]
        y = y + rows * wdw_ref[k:k + 1, :]
    mu = jnp.mean(y, axis=-1, keepdims=True)
    yc = y - mu
    var = jnp.mean(yc * yc, axis=-1, keepdims=True)
    yn = yc * lax.rsqrt(var + EPS) * lng_ref[...] + lnb_ref[...]
    ya = yn * jax.nn.sigmoid(yn)
    uo_ref[0] = jnp.dot(ya.astype(BF16), wpw_ref[...], preferred_element_type=F32)
    so_ref[0] = ext_ref[tm + pad:tm + CONV_HALO, :]
    carry = ext_ref[tm:tm + CONV_HALO, :]
    ext_ref[0:CONV_HALO, :] = carry


def _conv_kernel(u_ref, *refs):
    _conv_tile(u_ref[0], *refs)


def _conv_operands(state, prm, tm):
    const = lambda shape: pl.BlockSpec(shape, lambda b, t: (0,) * len(shape))
    operands = (state, prm["w_dw"], prm["b_dw"].reshape(1, -1), prm["conv_ln_g"].reshape(1, -1),
                prm["conv_ln_b"].reshape(1, -1), prm["w_pw2b"])
    in_specs = [pl.BlockSpec((1, CONV_K - 1, D_CONV), lambda b, t: (b, 0, 0)),
                const((CONV_K, D_CONV)), const((1, D_CONV)), const((1, D_CONV)), const((1, D_CONV)),
                const((D_CONV, D_MODEL))]
    out_specs = [pl.BlockSpec((1, tm, D_MODEL), lambda b, t: (b, t, 0)),
                 pl.BlockSpec((1, CONV_K - 1, D_CONV), lambda b, t: (b, 0, 0))]
    scratch = [pltpu.VMEM((CONV_HALO + tm, D_CONV), F32),
               pltpu.VMEM((SUBLANES - 1, tm + CONV_HALO - SUBLANES, D_CONV), F32)]
    return operands, in_specs, out_specs, scratch


def _conv(u, state, prm, tm):
    B, T, _ = u.shape
    operands, in_specs, out_specs, scratch = _conv_operands(state, prm, tm)
    return pl.pallas_call(
        _conv_kernel,
        grid=(B, T // tm),
        in_specs=[pl.BlockSpec((1, tm, D_CONV), lambda b, t: (b, t, 0))] + in_specs,
        out_specs=out_specs,
        out_shape=[jax.ShapeDtypeStruct((B, T, D_MODEL), F32), jax.ShapeDtypeStruct((B, CONV_K - 1, D_CONV), F32)],
        scratch_shapes=scratch,
        compiler_params=_cparams(("parallel", "arbitrary")),
        name="conv",
    )(u, *operands)


GATHER_SLOTS = 3


def _gather_pages(pt_ref, srcs, buf_ref, sem_ref, pages_per_step):
    step = pl.program_id(0) * pl.num_programs(1) + pl.program_id(1)
    n_steps = pl.num_programs(0) * pl.num_programs(1)
    ahead = GATHER_SLOTS - 1

    def copies(s):
        slot = s % GATHER_SLOTS
        return [pltpu.make_async_copy(src.at[pt_ref[s * pages_per_step + i]], buf_ref.at[slot, a, i], sem_ref.at[slot])
                for a, src in enumerate(srcs) for i in range(pages_per_step)]

    def start(s):
        for n, c in enumerate(copies(s)):
            c.start(priority=n % 2)

    for d in range(ahead):

        @pl.when((step == 0) & (d < n_steps))
        def _():
            start(d)

    @pl.when(step + ahead < n_steps)
    def _():
        start(step + ahead)

    for c in copies(step):
        c.wait()
    slot = step % GATHER_SLOTS
    return [jnp.concatenate([buf_ref[slot, a, i] for i in range(pages_per_step)], axis=1) for a in range(len(srcs))]


def _compress_core(xk, xv, modk_ref, wk_ref, modv_ref, wv_ref, s_ref, kc_ref, vc_ref, acc_ref, group):
    slot = pl.program_id(1) % group

    @pl.when(slot == 0)
    def _():
        acc_ref[...] = jnp.zeros_like(acc_ref)

    place = s_ref[slot]
    rows = [(x * (1.0 + mod_ref[...])).astype(BF16) for x, mod_ref in ((xk, modk_ref), (xv, modv_ref))]
    sums = jnp.dot(jnp.concatenate(rows, axis=0), place, preferred_element_type=F32)
    for i, (w_ref, out_ref) in enumerate(((wk_ref, kc_ref), (wv_ref, vc_ref))):
        acc_ref[i] += sums[i * KV_W:(i + 1) * KV_W]
        out_ref[0] = jnp.dot(w_ref[...], acc_ref[i].astype(BF16), preferred_element_type=F32)


def _compress_kernel(k_ref, v_ref, *refs, group):
    _compress_core(k_ref[0], v_ref[0], *refs, group=group)


def _compress_paged_kernel(pt_ref, k_hbm, v_hbm, *refs, group, pages_per_step):
    *core, buf_ref, sem_ref = refs
    xk, xv = _gather_pages(pt_ref, (k_hbm, v_hbm), buf_ref, sem_ref, pages_per_step)
    _compress_core(xk, xv, *core, group=group)


def _compress_consts(prm, cols):
    nbs = max(cols // BLOCK, 1)
    group = LANES // nbs
    modT = lambda m: jnp.tile(m.T, (N_KV, max(cols // BLOCK, 1)))[:, :cols]
    wT = lambda w: jnp.kron(jnp.eye(N_KV, dtype=F32), w.T).astype(BF16)
    r = jnp.arange(cols)[None, :, None]
    g = jnp.arange(group)[:, None, None]
    lane = jnp.arange(LANES)[None, None, :]
    place = jnp.where(lane == g * nbs + r // BLOCK, 1.0 / BLOCK, 0.0).astype(BF16)
    return (modT(prm["cmp_mod_k"]), wT(prm["cmp_w_k"]), modT(prm["cmp_mod_v"]), wT(prm["cmp_w_v"]), place), group


def _compress_specs(cols, group, n_batch, n_out_blocks):
    const = lambda shape: pl.BlockSpec(shape, lambda *a: (0,) * len(shape))
    ospec = pl.BlockSpec((1, KV_W, LANES), lambda b, j, *a: (b, 0, j // group))
    osh = jax.ShapeDtypeStruct((n_batch, KV_W, n_out_blocks * LANES), F32)
    const_specs = [const((KV_W, cols)), const((KV_W, KV_W)), const((KV_W, cols)), const((KV_W, KV_W)),
                   const((group, cols, LANES))]
    return const_specs, [ospec, ospec], [osh, osh], [pltpu.VMEM((2, KV_W, LANES), F32)]


def _compress(kT, vT, prm, cols):
    B, _, L = kT.shape
    cols = min(cols, L)
    n_blocks = -(-(L // BLOCK) // LANES) if L >= BLOCK else 1
    consts, group = _compress_consts(prm, cols)
    const_specs, out_specs, out_shape, scratch = _compress_specs(cols, group, B, n_blocks)
    x_spec = pl.BlockSpec((1, KV_W, cols), lambda b, j: (b, 0, j))
    return pl.pallas_call(
        functools.partial(_compress_kernel, group=group), grid=(B, L // cols),
        in_specs=[x_spec, x_spec] + const_specs, out_specs=out_specs, out_shape=out_shape, scratch_shapes=scratch,
        compiler_params=_cparams(("parallel", "arbitrary")), name="compress",
    )(kT, vT, *consts)


def _compress_paged(cache_kT, cache_vT, page_table, prm, pages_per_step):
    B, n_pages = page_table.shape
    cols = pages_per_step * PAGE
    n_blocks = -(-(n_pages * PAGE // BLOCK) // LANES)
    consts, group = _compress_consts(prm, cols)
    const_specs, out_specs, out_shape, scratch = _compress_specs(cols, group, B, n_blocks)
    hbm = pl.BlockSpec(memory_space=pl.ANY)
    grid_spec = pltpu.PrefetchScalarGridSpec(
        num_scalar_prefetch=1, grid=(B, n_pages // pages_per_step),
        in_specs=[hbm, hbm] + const_specs, out_specs=out_specs,
        scratch_shapes=scratch + [pltpu.VMEM((GATHER_SLOTS, 2, pages_per_step, KV_W, PAGE), F32),
                                  pltpu.SemaphoreType.DMA((GATHER_SLOTS,))])
    return pl.pallas_call(
        functools.partial(_compress_paged_kernel, group=group, pages_per_step=pages_per_step),
        grid_spec=grid_spec, out_shape=out_shape,
        compiler_params=_cparams(("arbitrary", "arbitrary")), name="compress_paged",
    )(page_table.reshape(-1), cache_kT, cache_vT, *consts)


def _attend_scores(s, vT, mask, carry, s_max=None):
    m, acc = carry
    if mask is not None:
        s = jnp.where(mask, s, NEG)
    if s_max is None:
        s_max = jnp.max(s, axis=0, keepdims=True)
    m_new = jnp.maximum(m, s_max)
    alpha = jnp.exp2(m - m_new)
    p = jnp.exp2(s - m_new).astype(BF16)
    return m_new, alpha * acc + jnp.dot(vT, p, preferred_element_type=F32)


def _attend(qT, k, vT, mask, carry):
    return _attend_scores(jnp.dot(k, qT, preferred_element_type=F32), vT, mask, carry)


def _attend_init(L, rows=V_AUG):
    return jnp.full((1, L), M_INIT, F32), jnp.zeros((rows, L), F32)


def _attend_out(carry):
    _, acc = carry
    return acc[:HEAD_DIM] / jnp.maximum(acc[HEAD_DIM:HEAD_DIM + 1], 1e-30)


def _with_ones(vT):
    return jnp.concatenate([vT, jnp.ones((ONES_ROWS, vT.shape[1]), BF16)], axis=0)


def _block_mask(sel_rows, kc):
    nb, L = sel_rows.shape
    return jnp.broadcast_to((sel_rows > 0.5)[:, None, :], (nb, BLOCK, L)).reshape(kc, L)


def _cmp_and_select(qpT, kc, vcT, q0, tq):
    return _cmp_finish(jnp.dot(kc, qpT, preferred_element_type=F32), vcT, q0, tq)


def _cmp_finish(s, vcT, q0, tq):
    nb, L = s.shape
    blk = lax.broadcasted_iota(jnp.int32, (nb, L), 0)
    qpos = q0 + lax.broadcasted_iota(jnp.int32, (1, L), 1) % tq
    ok = (blk + 1) * BLOCK <= qpos + 1
    s = jnp.where(ok, s, -jnp.inf)
    m = jnp.max(s, axis=0, keepdims=True)
    m = jnp.where(m > -jnp.inf, m, 0.0)
    e = jnp.where(ok, jnp.exp2(s - m), 0.0)
    p = e / jnp.maximum(jnp.sum(e, axis=0, keepdims=True), 1e-30)
    o_cmp = jnp.dot(vcT, p.astype(BF16), preferred_element_type=F32)
    if tq % LANES == 0:
        w = tq
        imp = p[:, :tq]
        for g in range(1, GROUP):
            imp = imp + p[:, g * tq:(g + 1) * tq]
    else:
        w = L
        seg = GROUP * tq
        in_seg = lax.broadcasted_iota(jnp.int32, (1, L), 1) % seg
        imp = p
        for g in range(1, GROUP):
            rolled = pltpu.roll(p, g * tq, 1)
            if seg != L:
                rolled = jnp.where(in_seg >= g * tq, rolled, pltpu.roll(p, (g * tq - seg) % L, 1))
            imp = imp + rolled
    if w != L:
        blk = lax.broadcasted_iota(jnp.int32, (nb, w), 0)
        qpos = q0 + lax.broadcasted_iota(jnp.int32, (1, w), 1)
    cur = lax.shift_right_logical(qpos, 6)
    visible = blk <= cur
    forced = (blk == 0) | (blk == cur) | (blk == cur - 1)
    score = jnp.where(visible, imp + jnp.where(forced, FORCED_BONUS, 0.0), -1.0)
    ridx = blk.astype(F32)
    sel = jnp.zeros_like(score)
    rem = score
    for _ in range(min(N_SEL, nb)):
        mx = jnp.max(rem, axis=0, keepdims=True)
        first = jnp.min(jnp.where(rem == mx, ridx, float(nb)), axis=0, keepdims=True)
        pick = ridx == first
        rem = jnp.where(pick, -jnp.inf, rem)
        sel = jnp.where(pick, 1.0, sel)
    if w != L:
        sel = jnp.concatenate([sel] * GROUP, axis=1)
    return o_cmp, sel


def _nsa_prompt_kernel(qp_ref, qr_ref, kc_ref, vcT_ref, ksel_ref, vselT_ref, kwin_ref, vwinT_ref, ngT_ref,
                       o_ref, bias_ref, s_ref, smax_ref, sel_ref, win_ref, diag_ref, *, tq):
    t = pl.program_id(1)
    s0 = t * tq
    L = GROUP * tq
    lane = lax.broadcasted_iota(jnp.int32, (1, L), 1)
    qpos = s0 + lane % tq
    ng = ngT_ref[0]
    heads = [slice(h * HEAD_DIM, (h + 1) * HEAD_DIM) for h in range(N_KV)]

    def qpad(ref, h):
        q = jnp.concatenate([ref[0, (h * GROUP + g) * HEAD_DIM:(h * GROUP + g + 1) * HEAD_DIM, :]
                             for g in range(GROUP)], axis=1)
        z = jnp.zeros_like(q)
        return jnp.concatenate([q, z] if h == 0 else [z, q], axis=0)

    qr = [qpad(qr_ref, h) for h in range(N_KV)]
    vrows = [slice(h * V_AUG, (h + 1) * V_AUG) for h in range(N_KV)]
    n_blk = SEL_CHUNK // BLOCK
    n_full = s0 // SEL_CHUNK
    k_diag = pl.multiple_of(n_full * SEL_CHUNK, SEL_CHUNK)
    w0 = pl.multiple_of(jnp.maximum(s0 - WINDOW, 0), tq)

    s_cmp = [jnp.dot(kc_ref[0], qpad(qp_ref, h), preferred_element_type=F32) for h in range(N_KV)]
    for h in range(N_KV):
        win_ref[h] = jnp.dot(kwin_ref[0, pl.ds(w0, WINDOW + tq), :], qr[h], preferred_element_type=F32)
        diag_ref[h] = jnp.dot(ksel_ref[0, pl.ds(k_diag, SEL_CHUNK), pl.ds(0, KV_W)], qr[h], preferred_element_type=F32)

    o_cmp = []
    for h in range(N_KV):
        o, sel = _cmp_finish(s_cmp[h], vcT_ref[0, heads[h], :], s0, tq)
        o_cmp.append(o)
        sel_ref[h] = sel
        bias = jnp.where(sel > 0.5, 0.0, NEG).reshape(-1, n_blk, L)
        bias = jnp.concatenate([bias, jnp.full((1, n_blk, L), NEG, F32)], axis=0)
        bias_ref[h] = jnp.concatenate([bias, jnp.zeros_like(bias)], axis=1).astype(BF16)

    dead = bias_ref.shape[1] - 1

    def scores(c, bias_idx, h):
        k0 = pl.multiple_of(c * SEL_CHUNK, SEL_CHUNK)
        return jnp.dot(ksel_ref[0, pl.ds(k0, SEL_CHUNK), :], jnp.concatenate([qr[h], bias_ref[h, bias_idx]], axis=0),
                       preferred_element_type=F32)

    def issue(c, slot):
        c_eff = jnp.minimum(c, jnp.maximum(n_full - 1, 0))
        for h in range(N_KV):
            s = scores(c_eff, jnp.where(c < n_full, c, dead), h)
            s_ref[slot, h] = s
            smax_ref[slot, h] = jnp.max(s, axis=0, keepdims=True)

    def consume(c, slot, carries):
        k0 = pl.multiple_of(jnp.minimum(c, jnp.maximum(n_full - 1, 0)) * SEL_CHUNK, SEL_CHUNK)
        return tuple(_attend_scores(s_ref[slot, h], vselT_ref[0, vrows[h], pl.ds(k0, SEL_CHUNK)], None, carries[h],
                                    smax_ref[slot, h]) for h in range(N_KV))


    def window_out(whole_window):
        issue(0, 0)
        outs = []
        for h in range(N_KV):
            s = win_ref[h]
            if whole_window:
                r = lax.broadcasted_iota(jnp.int32, (tq, L), 0)
                qi = lane % tq
                s = jnp.concatenate([jnp.where(r > qi, s[:tq], NEG), s[tq:WINDOW], jnp.where(r <= qi, s[WINDOW:], NEG)],
                                    axis=0)
            else:
                kpos = w0 + lax.broadcasted_iota(jnp.int32, (WINDOW + tq, L), 0)
                s = jnp.where((kpos <= qpos) & (kpos > qpos - WINDOW), s, NEG)
            outs.append(_attend_out(_attend_scores(s, vwinT_ref[0, vrows[h], pl.ds(w0, WINDOW + tq)], None,
                                                   _attend_init(L))))
        return tuple(outs)

    o_win = lax.cond(s0 >= WINDOW, lambda: window_out(True), lambda: window_out(False))

    r_diag = pl.multiple_of(n_full * n_blk, n_blk)

    def diag_arm(n_rows):
        def arm():
            causal = k_diag + lax.broadcasted_iota(jnp.int32, (n_rows, L), 0) <= qpos
            return tuple(_attend_scores(diag_ref[h, :n_rows], vselT_ref[0, vrows[h], pl.ds(k_diag, n_rows)],
                                        _block_mask(sel_ref[h, pl.ds(r_diag, n_rows // BLOCK), :], n_rows) & causal,
                                        _attend_init(L)) for h in range(N_KV))
        return arm

    carries = lax.switch((s0 % SEL_CHUNK) // tq, [diag_arm(tq * (j + 1)) for j in range(SEL_CHUNK // tq)])

    def issue_and_consume(c_issue, slot_i, c_cons, slot_c, carries):
        last = jnp.maximum(n_full - 1, 0)
        ki = pl.multiple_of(jnp.minimum(c_issue, last) * SEL_CHUNK, SEL_CHUNK)
        kc = pl.multiple_of(jnp.minimum(c_cons, last) * SEL_CHUNK, SEL_CHUNK)
        bias_idx = jnp.where(c_issue < n_full, c_issue, dead)
        slab = SEL_CHUNK // 2
        m_new = [jnp.maximum(carries[h][0], smax_ref[slot_c, h]) for h in range(N_KV)]
        acc = [jnp.exp2(carries[h][0] - m_new[h]) * carries[h][1] for h in range(N_KV)]
        q_aug = [jnp.concatenate([qr[h], bias_ref[h, bias_idx]], axis=0) for h in range(N_KV)]
        s_max = [None] * N_KV
        for r in range(SEL_CHUNK // slab):
            rows = slice(r * slab, (r + 1) * slab)
            for h in range(N_KV):
                s = jnp.dot(ksel_ref[0, pl.ds(ki + r * slab, slab), :], q_aug[h], preferred_element_type=F32)
                s_ref[slot_i, h, rows, :] = s
                mx = jnp.max(s, axis=0, keepdims=True)
                s_max[h] = mx if s_max[h] is None else jnp.maximum(s_max[h], mx)
                p = jnp.exp2(s_ref[slot_c, h, rows, :] - m_new[h]).astype(BF16)
                acc[h] = acc[h] + jnp.dot(vselT_ref[0, vrows[h], pl.ds(kc + r * slab, slab)], p,
                                          preferred_element_type=F32)
        for h in range(N_KV):
            smax_ref[slot_i, h] = s_max[h]
        return tuple((m_new[h], acc[h]) for h in range(N_KV))

    def pair(i, carries):
        c = 2 * i
        carries = issue_and_consume(c + 1, 1, c, 0, carries)
        return issue_and_consume(c + 2, 0, c + 1, 1, carries)

    carries = lax.fori_loop(0, (n_full + 1) // 2, pair, carries)

    def gate(h, br):
        return jnp.concatenate([ng[h * 12 + g * 3 + br:h * 12 + g * 3 + br + 1, :] for g in range(GROUP)], axis=1)

    for h in range(N_KV):
        o_sel = _attend_out(carries[h])
        o = gate(h, 0) * o_cmp[h] + gate(h, 1) * o_sel + gate(h, 2) * o_win[h]
        oT = jnp.concatenate([o[:, g * tq:(g + 1) * tq] for g in range(GROUP)], axis=0)
        o_ref[0, :, h * GROUP * HEAD_DIM:(h + 1) * GROUP * HEAD_DIM] = oT.T.astype(o_ref.dtype)


def _nsa_prompt(f, kcb, vcTb, tq):
    B, _, T = f["qp"].shape
    assert T % SEL_CHUNK == 0 and SEL_CHUNK % tq == 0 and WINDOW % tq == 0 and T >= WINDOW + tq
    nbp = kcb.shape[1]
    full = lambda shape: pl.BlockSpec((1,) + shape, lambda b, t: (b, 0, 0))
    return pl.pallas_call(
        functools.partial(_nsa_prompt_kernel, tq=tq),
        grid=(B, T // tq),
        in_specs=[pl.BlockSpec((1, Q_W, tq), lambda b, t: (b, 0, t)), pl.BlockSpec((1, Q_W, tq), lambda b, t: (b, 0, t)),
                  full((nbp, KV_W)), full((KV_W, nbp)),
                  full((T, K_AUG)), full((N_KV * V_AUG, T)), full((T, KV_W)), full((N_KV * V_AUG, T)),
                  pl.BlockSpec((1, 32, tq), lambda b, t: (b, 0, t))],
        out_specs=pl.BlockSpec((1, tq, Q_W), lambda b, t: (b, t, 0)),
        out_shape=jax.ShapeDtypeStruct((B, T, Q_W), BF16),
        scratch_shapes=[pltpu.VMEM((N_KV, nbp // (SEL_CHUNK // BLOCK) + 1, 2 * (SEL_CHUNK // BLOCK), GROUP * tq), BF16),
                        pltpu.VMEM((2, N_KV, SEL_CHUNK, GROUP * tq), F32), pltpu.VMEM((2, N_KV, 1, GROUP * tq), F32),
                        pltpu.VMEM((N_KV, nbp, GROUP * tq), F32),
                        pltpu.VMEM((N_KV, WINDOW + tq, GROUP * tq), F32),
                        pltpu.VMEM((N_KV, SEL_CHUNK, GROUP * tq), F32)],
        compiler_params=_cparams(("parallel", "arbitrary")),
        name="nsa_prompt",
    )(f["qp"], f["qr"], kcb, vcTb, f["kselb"], f["vselTb"], f["kwinb"], f["vwinTb"], f["ngT"])


def _two_head_values(vT):
    ones = jnp.ones((ONES_ROWS, vT.shape[1]), BF16)
    return jnp.concatenate([vT[:HEAD_DIM], ones, vT[HEAD_DIM:], ones], axis=0)


def _own_head_rows(x, lane_head):
    r = x.shape[0] // N_KV
    return jnp.where(lane_head == 0, x[:r], x[r:])


def _nsa_sample_kernel(pt_ref, qp_ref, qr_ref, kc_ref, vcT_ref, gate_ref, k_hbm, v_hbm,
                       knew_ref, vnewT_ref, cwkT_ref, cwvT_ref, kwnew_ref, vwnewT_ref,
                       o_ref, sel_ref, ocmp_ref, m_ref, acc_ref, buf_ref, sem_ref, *, tqs, past_len, pages_per_step):
    kT_pages, vT_pages = _gather_pages(pt_ref, (k_hbm, v_hbm), buf_ref, sem_ref, pages_per_step)
    j = pl.program_id(1)
    n_steps = pl.num_programs(1)
    L = N_KV * GROUP * tqs
    kc_step = pages_per_step * PAGE
    lane = lax.broadcasted_iota(jnp.int32, (1, L), 1)
    qi = lane % tqs
    lane_head = lane // (GROUP * tqs)

    def out(carry):
        acc = _own_head_rows(carry[1], lane_head)
        return acc[:HEAD_DIM] / jnp.maximum(acc[HEAD_DIM:HEAD_DIM + 1], 1e-30)

    @pl.when(j == 0)
    def _():
        o_cmp, sel = _cmp_and_select(qp_ref[0], kc_ref[0], vcT_ref[0], past_len, tqs)
        sel_ref[...] = sel
        ocmp_ref[...] = _own_head_rows(o_cmp, lane_head)
        m_ref[...], acc_ref[...] = _attend_init(L, N_KV * V_AUG)

    k = kT_pages.T.astype(BF16)
    r0 = pl.multiple_of(j * (kc_step // BLOCK), kc_step // BLOCK)
    mask = _block_mask(sel_ref[pl.ds(r0, kc_step // BLOCK), :], kc_step)
    vT = _two_head_values(vT_pages.astype(BF16))
    half = kc_step // 2
    scores = [jnp.dot(k[i * half:(i + 1) * half], qr_ref[0], preferred_element_type=F32) for i in range(2)]
    carry = (m_ref[...], acc_ref[...])
    for i in range(2):
        carry = _attend_scores(scores[i], vT[:, i * half:(i + 1) * half], mask[i * half:(i + 1) * half], carry)
    m_ref[...], acc_ref[...] = carry

    @pl.when(j == n_steps - 1)
    def _():
        nb_past = past_len // BLOCK
        buf = cwkT_ref.shape[2]
        r_new = lax.broadcasted_iota(jnp.int32, (tqs, L), 0)
        r_buf = lax.broadcasted_iota(jnp.int32, (buf, L), 0)
        qr = qr_ref[0]
        mask_new = (sel_ref[nb_past:nb_past + 1, :] > 0.5) & (r_new <= qi)
        o_sel = out(_attend(qr, knew_ref[0], _two_head_values(vnewT_ref[0]), mask_new, (m_ref[...], acc_ref[...])))
        carry = _attend(qr, cwkT_ref[0].T.astype(BF16), _two_head_values(cwvT_ref[0].astype(BF16)),
                        r_buf > qi + (buf - WINDOW), _attend_init(L, N_KV * V_AUG))
        o_win = out(_attend(qr, kwnew_ref[0], _two_head_values(vwnewT_ref[0]), r_new <= qi, carry))
        g = gate_ref[0]
        o_ref[0] = g[0:1] * ocmp_ref[...] + g[1:2] * o_sel + g[2:3] * o_win


def _nsa_sample(qpT, qrT, kcb, vcTb, gates, cache_sel_kT, cache_sel_vT, page_table, knew, vnewT,
                cache_win_kT, cache_win_vT, kwnew, vwnewT, tqs, pages_per_step):
    Bs, n_pages = page_table.shape
    L = N_KV * GROUP * tqs
    nbp = kcb.shape[1]
    buf = cache_win_kT.shape[2]
    per_b = lambda shape: pl.BlockSpec((1,) + shape, lambda b, j, pt: (b,) + (0,) * len(shape))
    hbm = pl.BlockSpec(memory_space=pl.ANY)
    grid_spec = pltpu.PrefetchScalarGridSpec(
        num_scalar_prefetch=1,
        grid=(Bs, n_pages // pages_per_step),
        in_specs=[per_b((KV_W, L)), per_b((KV_W, L)), per_b((nbp, KV_W)), per_b((KV_W, nbp)), per_b((8, L))]
                 + [hbm, hbm]
                 + [per_b((tqs, KV_W)), per_b((KV_W, tqs)), per_b((KV_W, buf)), per_b((KV_W, buf)),
                    per_b((tqs, KV_W)), per_b((KV_W, tqs))],
        out_specs=per_b((HEAD_DIM, L)),
        scratch_shapes=[pltpu.VMEM((nbp, L), F32), pltpu.VMEM((HEAD_DIM, L), F32),
                        pltpu.VMEM((1, L), F32), pltpu.VMEM((N_KV * V_AUG, L), F32),
                        pltpu.VMEM((GATHER_SLOTS, 2, pages_per_step, KV_W, PAGE), F32),
                        pltpu.SemaphoreType.DMA((GATHER_SLOTS,))],
    )
    return pl.pallas_call(
        functools.partial(_nsa_sample_kernel, tqs=tqs, past_len=n_pages * PAGE, pages_per_step=pages_per_step),
        grid_spec=grid_spec,
        out_shape=jax.ShapeDtypeStruct((Bs, HEAD_DIM, L), F32),
        compiler_params=_cparams(("arbitrary", "arbitrary")),
        name="nsa_sample",
    )(page_table.reshape(-1), qpT, qrT, kcb, vcTb, gates, cache_sel_kT, cache_sel_vT,
      knew, vnewT, cache_win_kT, cache_win_vT, kwnew, vwnewT)


def _back_kernel(x_ref, o_ref, uo_ref, gmg_ref, g1_ref, sh2_ref, sc2_ref, g2_ref, fg_ref,
                 wno_ref, wout_ref, wg_ref, wu_ref, wd_ref, y_ref):
    nsa = jnp.dot(o_ref[0], wno_ref[...], preferred_element_type=F32)
    gmg = gmg_ref[0]
    merged = gmg[:, :D_MODEL] * uo_ref[0] + gmg[:, D_MODEL:] * nsa
    mix = jnp.dot(merged.astype(BF16), wout_ref[...], preferred_element_type=F32)
    x1 = x_ref[0] + g1_ref[0] * mix
    ms = jnp.mean(x1 * x1, axis=-1, keepdims=True)
    h = x1 * lax.rsqrt(ms + EPS) * fg_ref[...]
    h = h * (1.0 + sc2_ref[0]) + sh2_ref[0]
    hb = h.astype(BF16)
    a = jnp.dot(hb, wg_ref[...], preferred_element_type=F32)
    b = jnp.dot(hb, wu_ref[...], preferred_element_type=F32)
    act = a * jax.nn.sigmoid(a) * b
    ffn = jnp.dot(act.astype(BF16), wd_ref[...], preferred_element_type=F32)
    y_ref[0] = x1 + g2_ref[0] * ffn


def _back(x, o, u_out, gmg, mod, prm, tm):
    B, T, _ = x.shape
    Tm = mod.shape[1]
    tmm = 1 if Tm == 1 else tm
    tok = lambda w: pl.BlockSpec((1, tm, w), lambda b, t: (b, t, 0))
    modspec = lambda k: pl.BlockSpec((1, tmm, D_MODEL), (lambda b, t: (b, t, k)) if Tm != 1 else (lambda b, t: (b, 0, k)))
    wspec = lambda shape: pl.BlockSpec(shape, lambda b, t: (0, 0), pipeline_mode=pl.Buffered(1))
    return pl.pallas_call(
        _back_kernel,
        grid=(B, T // tm),
        in_specs=[tok(D_MODEL), tok(Q_W), tok(D_MODEL), tok(2 * D_MODEL),
                  modspec(2), modspec(3), modspec(4), modspec(5), wspec((1, D_MODEL)),
                  wspec((Q_W, D_MODEL)), wspec((D_MODEL, D_MODEL)), wspec((D_MODEL, D_FF)), wspec((D_MODEL, D_FF)),
                  wspec((D_FF, D_MODEL))],
        out_specs=tok(D_MODEL),
        out_shape=jax.ShapeDtypeStruct((B, T, D_MODEL), F32),
        compiler_params=_cparams(("parallel", "arbitrary")),
        name="back",
    )(x, o, u_out, gmg, mod, mod, mod, mod, prm["ffn_norm_g"].reshape(1, -1),
      prm["w_nsa_ob"], prm["w_outb"], prm["w_gateb"], prm["w_upb"], prm["w_downb"])


def _prep_params(p):
    w_in = p["w_in"]
    q = dict(p)
    q["wn"] = jnp.concatenate([w_in[:, :OFF_Q], w_in[:, OFF_MG:]], axis=1).astype(BF16)
    wt = jnp.concatenate([w_in[:, OFF_Q:OFF_MG], jnp.zeros((D_MODEL, N_T_ROWS - (OFF_MG - OFF_Q)), F32)], axis=1)
    q["wt"] = wt.T.astype(BF16)
    q["w_pw2b"] = p["w_pw2"].astype(BF16)
    for name in ("w_nsa_o", "w_out", "w_gate", "w_up", "w_down"):
        q[name + "b"] = p[name].astype(BF16)
    return q


def _rows_from_channel_major(a):
    B, _, T = a.shape
    return jnp.transpose(a.reshape(B, N_KV, HEAD_DIM, T), (0, 3, 1, 2))


def _channel_major(a):
    N, R = a.shape[:2]
    return jnp.transpose(a, (0, 2, 3, 1)).reshape(N, KV_W, R)


def _prompt_layer(x, mod, prm, tm, tq):
    B, T, _ = x.shape
    mod = mod[:, None, :]
    f = _front(x, mod, jnp.arange(T, dtype=jnp.int32), prm, tm)
    u_out, conv_state = _conv(f["u"], jnp.zeros((B, CONV_K - 1, D_CONV), F32), prm, tm)
    kcT, vcT = _compress(f["kcmpT"], f["vcmpT"], prm, 2048)
    o = _nsa_prompt(f, jnp.swapaxes(kcT, 1, 2).astype(BF16), vcT.astype(BF16), tq)
    y = _back(x, o, u_out, f["gmg"], mod, prm, tm)
    keep = min(WINDOW, T)
    rows = _rows_from_channel_major
    return y, (rows(f["kcmpT"]), rows(f["vcmpT"]), rows(f["kselT"]), rows(f["vselT"]),
               rows(f["kwinT"][:, :, T - keep:]), rows(f["vwinT"][:, :, T - keep:]), conv_state)


def _sample_layer(x, mod, cache_cmp_k, cache_cmp_v, cache_sel_k, cache_sel_v, cache_win_k, cache_win_v,
                  state_conv, page_table, prm):
    Bs, Ts, _ = x.shape
    n_pages = page_table.shape[1]
    past_len = n_pages * PAGE
    n_tok = Bs * Ts
    tqs = LANES // (N_KV * GROUP)
    assert n_tok % LANES == 0 and Ts <= tqs and n_pages % min(16, n_pages) == 0
    pos = past_len + jnp.arange(n_tok, dtype=jnp.int32) % Ts
    mod_tok = jnp.repeat(mod, Ts, axis=0)[None]
    f = _front(x.reshape(1, n_tok, D_MODEL), mod_tok, pos, prm, n_tok)
    u_out, conv_state = _conv(f["u"].reshape(Bs, Ts, D_CONV), state_conv, prm, Ts)

    def lanes(a, rows, width):
        a = a.reshape(rows + (Bs, Ts))
        a = jnp.moveaxis(a, len(rows), 0)
        return jnp.pad(a, ((0, 0),) * (len(rows) + 1) + ((0, width - Ts),))

    kcT_past, vcT_past = _compress_paged(_channel_major(cache_cmp_k), _channel_major(cache_cmp_v), page_table, prm,
                                         min(32, n_pages))
    kcT_new, vcT_new = _compress(lanes(f["kcmpT"][0], (KV_W,), BLOCK), lanes(f["vcmpT"][0], (KV_W,), BLOCK), prm, BLOCK)
    nb_past = past_len // BLOCK
    cat = lambda a, b: jnp.concatenate([a[:, :, :nb_past], b], axis=2)
    kcb = jnp.swapaxes(cat(kcT_past, kcT_new), 1, 2).astype(BF16)
    vcTb = cat(vcT_past, vcT_new).astype(BF16)

    def q_operand(qT):
        q = lanes(qT[0], (N_KV, GROUP, HEAD_DIM), tqs)
        q = jnp.moveaxis(q, 2, 3).reshape(Bs, N_KV, HEAD_DIM, GROUP * tqs)
        z = jnp.zeros_like(q[:, 0])
        return jnp.concatenate([jnp.concatenate([q[:, 0], z], axis=2), jnp.concatenate([z, q[:, 1]], axis=2)], axis=1)

    g = lanes(f["ngT"][0, :3 * N_HEADS], (N_KV, GROUP, 3), tqs)
    g = jnp.transpose(g, (0, 3, 1, 2, 4)).reshape(Bs, 3, N_KV * GROUP * tqs)
    gates = jnp.pad(g, ((0, 0), (0, 5), (0, 0)))
    rows_nat = lambda a: jnp.pad(a[..., :KV_W].reshape(Bs, Ts, KV_W), ((0, 0), (0, tqs - Ts), (0, 0)))
    rows_T = lambda a: lanes(a[0].astype(BF16), (KV_W,), tqs)
    oT = _nsa_sample(q_operand(f["qp"]), q_operand(f["qr"]), kcb, vcTb, gates,
                     _channel_major(cache_sel_k), _channel_major(cache_sel_v), page_table,
                     rows_nat(f["kselb"]), rows_T(f["vselT"]),
                     _channel_major(cache_win_k), _channel_major(cache_win_v),
                     rows_nat(f["kwinb"]), rows_T(f["vwinT"]), tqs, min(16, n_pages))
    o = oT.reshape(Bs, HEAD_DIM, N_KV, GROUP, tqs)[..., :Ts]
    o = jnp.transpose(o, (0, 4, 2, 3, 1)).reshape(1, n_tok, Q_W).astype(BF16)
    y = _back(x.reshape(1, n_tok, D_MODEL), o, u_out.reshape(1, n_tok, D_MODEL), f["gmg"], mod_tok, prm, n_tok)
    hs = lambda a: a[0].T.reshape(Bs, Ts, N_KV, HEAD_DIM)
    buf = cache_win_k.shape[1]
    keep = min(WINDOW, buf + Ts)
    kw = jnp.concatenate([cache_win_k, hs(f["kwinT"])], axis=1)[:, -keep:]
    vw = jnp.concatenate([cache_win_v, hs(f["vwinT"])], axis=1)[:, -keep:]
    return y.reshape(Bs, Ts, D_MODEL), (hs(f["kcmpT"]), hs(f["vcmpT"]), hs(f["kselT"]), hs(f["vselT"]), kw, vw, conv_state)


def kernel(x_prompt, x_sample, cache_cmp_k, cache_cmp_v, cache_sel_k, cache_sel_v, cache_win_k, cache_win_v,
           state_conv, page_table, c_prompt, c_sample, w_ada, b_ada, mix_norm_g, w_in, q_norm_g, k_norm_g,
           w_dw, b_dw, conv_ln_g, conv_ln_b, w_pw2, cmp_mod_k, cmp_w_k, cmp_mod_v, cmp_w_v, w_nsa_o, w_out,
           ffn_norm_g, w_gate, w_up, w_down):
    depth = w_in.shape[0]
    yp, ys = x_prompt, x_sample
    st_p, st_s = [], []
    for l in range(depth):
        p = dict(w_ada=w_ada[l], b_ada=b_ada[l], mix_norm_g=mix_norm_g[l], w_in=w_in[l],
                 q_norm_g=q_norm_g[l], k_norm_g=k_norm_g[l], w_dw=w_dw[l], b_dw=b_dw[l],
                 conv_ln_g=conv_ln_g[l], conv_ln_b=conv_ln_b[l], w_pw2=w_pw2[l],
                 cmp_mod_k=cmp_mod_k[l], cmp_w_k=cmp_w_k[l], cmp_mod_v=cmp_mod_v[l],
                 cmp_w_v=cmp_w_v[l], w_nsa_o=w_nsa_o[l], w_out=w_out[l],
                 ffn_norm_g=ffn_norm_g[l], w_gate=w_gate[l], w_up=w_up[l], w_down=w_down[l])
        prm = _prep_params(p)
        nb_p = c_prompt.shape[0]
        mod = _ada(jnp.concatenate([c_prompt, c_sample], axis=0), p["w_ada"], p["b_ada"])
        yp, sp = _prompt_layer(yp, mod[:nb_p], prm, 256, 128)
        ys, ss = _sample_layer(ys, mod[nb_p:], cache_cmp_k[l], cache_cmp_v[l], cache_sel_k[l], cache_sel_v[l],
                               cache_win_k[l], cache_win_v[l], state_conv[l], page_table, prm)
        st_p.append(sp)
        st_s.append(ss)
    stk = lambda lst, j: jnp.stack([s[j] for s in lst])
    return (yp, ys) + tuple(stk(st_p, j) for j in range(7)) + tuple(stk(st_s, j) for j in range(7))
```
